```python
import jax, jax.numpy as jnp
from jax import lax
import numpy as np

D_MODEL = 1024
BATCH = 4
SEQ = 4096
DEPTH = 2

GRID_W = 64
CTX_LEN = 256
D_A = D_MODEL
SGU_GROUPS = 8
SGU_GROUP_DIM = D_A // SGU_GROUPS
SGU_CHUNK = 128
D_B = D_MODEL
HGRN_HEAD_DIM = 128
HGRN_HEADS = D_B // HGRN_HEAD_DIM
HGRN_CHUNK = 64
D_FF = 2816
CONV_W = 3
N_MOD = 6
RMS_EPS = 1e-6
LN_EPS = 1e-5
IN_SPLITS = (D_B, 2 * D_B, 3 * D_B, 4 * D_B, 4 * D_B + D_A, 4 * D_B + 2 * D_A, 5 * D_B + 2 * D_A, 5 * D_B + 2 * D_A + D_MODEL)
D_IN = 5 * D_B + 2 * D_A + 2 * D_MODEL

kernel_name = 'hybrid_sgu_hgrn2_convffn_prefix_dit'


def rms_norm(x, w):
    xf = x.astype(jnp.float32)
    y = xf * lax.rsqrt(jnp.mean(xf * xf, axis=-1, keepdims=True) + RMS_EPS)
    return (y * w.astype(jnp.float32)).astype(x.dtype)


def layer_norm(x, w, b):
    xf = x.astype(jnp.float32)
    mu = jnp.mean(xf, axis=-1, keepdims=True)
    var = jnp.mean(jnp.square(xf - mu), axis=-1, keepdims=True)
    y = (xf - mu) * lax.rsqrt(var + LN_EPS)
    return (y * w.astype(jnp.float32) + b.astype(jnp.float32)).astype(x.dtype)


def modulate(h, shift, scale):
    return h * (1 + scale) + shift


def to_heads(t):
    bsz, length, _ = t.shape
    return t.reshape(bsz, length, HGRN_HEADS, HGRN_HEAD_DIM).transpose(0, 2, 1, 3)


def hgrn_forget(f_logit, lb):
    z = f_logit.astype(jnp.float32)
    f = lb + (1 - lb) * jax.nn.sigmoid(z)
    return to_heads((1 - lb) * jax.nn.sigmoid(-z)), to_heads(jnp.log(f))


def gla_chunked(q, k, v, g, s0):
    bsz, heads, length, _ = q.shape
    n = length // HGRN_CHUNK
    split = lambda t: t.reshape(bsz, heads, n, HGRN_CHUNK, t.shape[-1])
    q, k, v, g = split(q), split(k), split(v), split(g)
    b = jnp.cumsum(g, axis=3)
    b_last = b[:, :, :, -1:, :]
    ref = b[:, :, :, HGRN_CHUNK // 2 - 1:HGRN_CHUNK // 2, :]
    scores = jnp.einsum('bhntk,bhnsk->bhnts', q * jnp.exp(b - ref), k * jnp.exp(ref - b))
    lower = jnp.tril(jnp.ones((HGRN_CHUNK, HGRN_CHUNK), dtype=bool))
    o_intra = jnp.einsum('bhnts,bhnsv->bhntv', jnp.where(lower, scores, 0.0), v)
    q_inter = q * jnp.exp(b)
    kv = jnp.einsum('bhnsk,bhnsv->bhnkv', k * jnp.exp(b_last - b), v)
    decay = jnp.exp(b_last[:, :, :, 0, :])

    def step(state, xs):
        q_n, kv_n, d_n = xs
        o_n = jnp.einsum('bhtk,bhkv->bhtv', q_n, state)
        return d_n[..., None] * state + kv_n, o_n

    move = lambda t: jnp.moveaxis(t, 2, 0)
    s_final, o_inter = lax.scan(step, s0, (move(q_inter), move(kv), move(decay)))
    o = o_intra + jnp.moveaxis(o_inter, 0, 2)
    return o.reshape(bsz, heads, length, -1), s_final


def hgrn_bidir(q, f_fwd, f_bwd, i, lb_fwd, lb_bwd, s0_fwd, s0_bwd):
    qh = to_heads(jax.nn.silu(q.astype(jnp.float32)))
    ih = to_heads(i.astype(jnp.float32))
    k_f, g_f = hgrn_forget(f_fwd, lb_fwd)
    k_b, g_b = hgrn_forget(f_bwd, lb_bwd)
    o_f, s_f = gla_chunked(qh, k_f, ih, g_f, s0_fwd)
    rev = lambda t: jnp.flip(t, axis=2)
    o_b, s_b = gla_chunked(rev(qh), rev(k_b), rev(ih), rev(g_b), s0_bwd)
    return o_f + rev(o_b), s_f, s_b


def hgrn_readout(o, og, norm_w):
    o = o * lax.rsqrt(jnp.mean(o * o, axis=-1, keepdims=True) + RMS_EPS) * norm_w.astype(jnp.float32)
    bsz, _, length, _ = o.shape
    o = o.transpose(0, 2, 1, 3).reshape(bsz, length, D_B).astype(og.dtype)
    return o * jax.nn.silu(og)


def sgu(u, v, ln_w, ln_b, w_s, b_s):
    bsz, length, _ = v.shape
    vn = layer_norm(v, ln_w, ln_b).reshape(bsz, length // SGU_CHUNK, SGU_CHUNK, SGU_GROUPS, SGU_GROUP_DIM)
    mixed = jnp.einsum('gts,bnsgd->bntgd', w_s, vn) + b_s.T[:, :, None]
    return u * mixed.reshape(bsz, length, D_A)


def token_mixer_out(parts, o_b, sgu_ln_w, sgu_ln_b, sgu_w, sgu_b, hgrn_norm_w, w_a, w_b, w_o):
    u, v, og, gate_a, gate_b = parts
    y_a = sgu(jax.nn.gelu(u), jax.nn.gelu(v), sgu_ln_w, sgu_ln_b, sgu_w, sgu_b)
    y_b = hgrn_readout(o_b, og, hgrn_norm_w)
    merged = jax.nn.sigmoid(gate_a) * (y_a @ w_a) + jax.nn.sigmoid(gate_b) * (y_b @ w_b)
    return merged @ w_o


def dwconv_grid(a, conv_w, conv_b):
    bsz, length, ch = a.shape
    rows = length // GRID_W
    y = lax.conv_general_dilated(a.reshape(bsz, rows, GRID_W, ch), conv_w[:, :, None, :].astype(a.dtype),
                                 window_strides=(1, 1), padding='SAME',
                                 dimension_numbers=('NHWC', 'HWIO', 'NHWC'), feature_group_count=ch)
    return y.reshape(bsz, length, ch) + conv_b


def dwconv_seq(a, conv_w, conv_b):
    y = lax.conv_general_dilated(a, conv_w[CONV_W // 2][:, None, :].astype(a.dtype),
                                 window_strides=(1,), padding='SAME',
                                 dimension_numbers=('NWC', 'WIO', 'NWC'), feature_group_count=a.shape[-1])
    return y + conv_b


def conv_ffn(h, w_up, conv_w, conv_b, w_down, on_grid):
    a, v = jnp.split(h @ w_up, 2, axis=-1)
    a = dwconv_grid(a, conv_w, conv_b) if on_grid else dwconv_seq(a, conv_w, conv_b)
    return (jax.nn.gelu(a) * v) @ w_down


def setup_inputs(seed: int = 0) -> dict:
    key = jax.random.key(seed)
    ks = jax.random.split(key, 24)
    nrm = lambda k, shape, s: s * jax.random.normal(k, shape, jnp.float32)
    gain = lambda k, shape: 1.0 + nrm(k, shape, 0.02)
    return {
        'x': nrm(ks[0], (BATCH, SEQ, D_MODEL), 1.0),
        'c': nrm(ks[1], (BATCH, D_MODEL), 1.0),
        'ctx': nrm(ks[2], (BATCH, CTX_LEN, D_MODEL), 1.0),
        'c_ctx': nrm(ks[3], (D_MODEL,), 1.0),
        'ada_w': nrm(ks[4], (DEPTH, D_MODEL, N_MOD * D_MODEL), 0.5 * D_MODEL ** -0.5),
        'ada_b': nrm(ks[5], (DEPTH, N_MOD * D_MODEL), 0.02),
        'norm1_w': gain(ks[6], (DEPTH, D_MODEL)),
        'w_in': nrm(ks[7], (DEPTH, D_MODEL, D_IN), D_MODEL ** -0.5),
        'sgu_ln_w': gain(ks[8], (DEPTH, D_A)),
        'sgu_ln_b': nrm(ks[9], (DEPTH, D_A), 0.02),
        'sgu_w': nrm(ks[10], (DEPTH, SGU_GROUPS, SGU_CHUNK, SGU_CHUNK), SGU_CHUNK ** -0.5),
        'sgu_b': gain(ks[11], (DEPTH, SGU_GROUPS, SGU_CHUNK)),
        'hgrn_lower_bounds': nrm(ks[12], (DEPTH, 2 * D_B), 0.1),
        'hgrn_norm_w': gain(ks[13], (DEPTH, HGRN_HEAD_DIM)),
        'w_branch_a': nrm(ks[14], (DEPTH, D_A, D_MODEL), D_A ** -0.5),
        'w_branch_b': nrm(ks[15], (DEPTH, D_B, D_MODEL), D_B ** -0.5),
        'w_out': nrm(ks[16], (DEPTH, D_MODEL, D_MODEL), D_MODEL ** -0.5),
        'norm2_w': gain(ks[17], (DEPTH, D_MODEL)),
        'ffn_w_up': nrm(ks[18], (DEPTH, D_MODEL, 2 * D_FF), D_MODEL ** -0.5),
        'ffn_conv_w': nrm(ks[19], (DEPTH, CONV_W, CONV_W, D_FF), 1.0 / CONV_W),
        'ffn_conv_b': nrm(ks[20], (DEPTH, D_FF), 0.02),
        'ffn_w_down': nrm(ks[21], (DEPTH, D_FF, D_MODEL), D_FF ** -0.5),
        'final_norm_w': gain(ks[22], (D_MODEL,)),
    }


def reference(x, c, ctx, c_ctx, ada_w, ada_b, norm1_w, w_in, sgu_ln_w, sgu_ln_b, sgu_w, sgu_b,
              hgrn_lower_bounds, hgrn_norm_w, w_branch_a, w_branch_b, w_out, norm2_w,
              ffn_w_up, ffn_conv_w, ffn_conv_b, ffn_w_down, final_norm_w):
    lb = jax.nn.softmax(hgrn_lower_bounds.astype(jnp.float32), axis=0)
    lb = jnp.cumsum(lb, axis=0) - lb[0]
    zero_state = jnp.zeros((ctx.shape[0], HGRN_HEADS, HGRN_HEAD_DIM, HGRN_HEAD_DIM), jnp.float32)
    for l in range(DEPTH):
        last = l == DEPTH - 1
        mod_x = (jax.nn.silu(c) @ ada_w[l] + ada_b[l])[:, None, :]
        mod_c = jax.nn.silu(c_ctx) @ ada_w[l] + ada_b[l]
        sh1, sc1, g1, sh2, sc2, g2 = jnp.split(mod_x, N_MOD, axis=-1)
        csh1, csc1, cg1, csh2, csc2, cg2 = jnp.split(mod_c, N_MOD, axis=-1)
        lb_f, lb_b = lb[l, :D_B], lb[l, D_B:]

        h_c = modulate(rms_norm(ctx, norm1_w[l]), csh1, csc1)
        n_cols = IN_SPLITS[3] if last else D_IN
        parts_c = jnp.split(h_c @ w_in[l, :, :n_cols], IN_SPLITS[:3] if last else IN_SPLITS, axis=-1)
        o_c, s_f, s_b = hgrn_bidir(*parts_c[:4], lb_f, lb_b, zero_state, zero_state)

        h_x = modulate(rms_norm(x, norm1_w[l]), sh1, sc1)
        parts_x = jnp.split(h_x @ w_in[l], IN_SPLITS, axis=-1)
        o_x, _, _ = hgrn_bidir(*parts_x[:4], lb_f, lb_b, s_f, s_b)
        x = x + g1 * token_mixer_out(parts_x[4:], o_x, sgu_ln_w[l], sgu_ln_b[l], sgu_w[l], sgu_b[l],
                                     hgrn_norm_w[l], w_branch_a[l], w_branch_b[l], w_out[l])
        h2 = modulate(rms_norm(x, norm2_w[l]), sh2, sc2)
        x = x + g2 * conv_ffn(h2, ffn_w_up[l], ffn_conv_w[l], ffn_conv_b[l], ffn_w_down[l], True)

        if not last:
            ctx = ctx + cg1 * token_mixer_out(parts_c[4:], o_c, sgu_ln_w[l], sgu_ln_b[l], sgu_w[l], sgu_b[l],
                                              hgrn_norm_w[l], w_branch_a[l], w_branch_b[l], w_out[l])
            hc2 = modulate(rms_norm(ctx, norm2_w[l]), csh2, csc2)
            ctx = ctx + cg2 * conv_ffn(hc2, ffn_w_up[l], ffn_conv_w[l], ffn_conv_b[l], ffn_w_down[l], False)
    return rms_norm(x, final_norm_w)
```

```python
import functools

import jax
import jax.numpy as jnp
from jax import lax
from jax.experimental import pallas as pl
from jax.experimental.pallas import tpu as pltpu

F32 = jnp.float32
BF16 = jnp.bfloat16

N_MOD = 6
RMS_EPS = 1e-6
LN_EPS = 1e-5
HEAD_DIM = 128
HGRN_CHUNK = 64
SGU_CHUNK = 128
SGU_GROUPS = 8
GRID_W = 64
CONV_W = 3

TOKEN_TILE = 256
FFN_TILE = 512
FFN_COLS = 256
MOD_COLS = 1536
VMEM_LIMIT_BYTES = 56 * 1024 * 1024


def _dot(a, b):
    return jnp.dot(a, b, preferred_element_type=F32)


def _dot_nt(a, b):
    return lax.dot_general(a, b, (((1,), (1,)), ((), ())), preferred_element_type=F32)


def _sigmoid(x):
    return 1.0 / (1.0 + jnp.exp(-x))


def _silu(x):
    return x * _sigmoid(x)


def _gelu_tanh(x):
    return 0.5 * x * (1.0 + jnp.tanh(0.7978845608028654 * (x + 0.044715 * (x * x * x))))


def _rms_mod(x, norm_w, shift, scale):
    y = x * lax.rsqrt(jnp.mean(x * x, axis=-1, keepdims=True) + RMS_EPS) * norm_w
    return y * (1.0 + scale) + shift


def _lower_bound_row(hlb, layer):
    depth = hlb.shape[0]
    rows = [hlb[j:j + 1] for j in range(depth)]
    mx = functools.reduce(jnp.maximum, rows)
    es = [jnp.exp(r - mx) for r in rows]
    tot = functools.reduce(lambda a, b: a + b, es)
    acc = jnp.zeros_like(rows[0])
    for j in range(1, layer + 1):
        acc = acc + es[j] / tot
    return acc


def _tri(reverse):
    r = lax.broadcasted_iota(jnp.int32, (HGRN_CHUNK, HGRN_CHUNK), 0)
    c = lax.broadcasted_iota(jnp.int32, (HGRN_CHUNK, HGRN_CHUNK), 1)
    return (c >= r) if reverse else (c <= r)


def _hgrn_chunk(q, z, v, lb, st_ref, reverse):
    d = q.shape[-1]
    mask = _tri(reverse)
    e = jnp.exp(-z)
    sig = 1.0 / (1.0 + e)
    f = lb + (1.0 - lb) * sig
    k = (1.0 - lb) * (e * sig)
    g = jnp.log(f)
    tri = mask.astype(BF16)
    g_hi = g.astype(BF16)
    g_lo = (g - g_hi.astype(F32)).astype(BF16)
    b = _dot(tri, g_hi) + _dot(tri, g_lo)
    half = HGRN_CHUNK // 2
    ref_i, last_i = (half, 0) if reverse else (half - 1, HGRN_CHUNK - 1)
    ref = b[ref_i:ref_i + 1]
    last = b[last_i:last_i + 1]
    qs = (q * jnp.exp(b - ref)).astype(BF16)
    ks = (k * jnp.exp(ref - b)).astype(BF16)
    q_inter = (q * jnp.exp(b)).astype(BF16)
    k_last = k * jnp.exp(last - b)
    decay = jnp.exp(last)
    vb = v.astype(BF16)
    outs = []
    for h in range(d // HEAD_DIM):
        sl = slice(h * HEAD_DIM, (h + 1) * HEAD_DIM)
        scores = jnp.where(mask, _dot_nt(qs[:, sl], ks[:, sl]), 0.0).astype(BF16)
        st = st_ref[h]
        outs.append(_dot(scores, vb[:, sl]) + _dot_nt(q_inter[:, sl], st.astype(BF16)))
        kv_t = _dot(v[:, sl].T.astype(BF16), k_last[:, sl].astype(BF16))
        st_ref[h] = st * decay[:, sl] + kv_t
    return jnp.concatenate(outs, axis=-1)


def _hgrn_scan(q_ref, z_ref, v_ref, o_ref, lb, st_ref, reverse):
    n_chunks = q_ref.shape[0] // HGRN_CHUNK

    def body(ci, carry):
        c = (n_chunks - 1 - ci) if reverse else ci
        rows = pl.ds(pl.multiple_of(c * HGRN_CHUNK, HGRN_CHUNK), HGRN_CHUNK)
        o_ref[rows, :] = _hgrn_chunk(q_ref[rows, :].astype(F32), z_ref[rows, :],
                                     v_ref[rows, :].astype(F32), lb, st_ref, reverse)
        return carry

    lax.fori_loop(0, n_chunks, body, 0)


def _mod_kernel(c_ref, w_ref, b_ref, o_ref):
    s = _silu(c_ref[...])
    w = w_ref[0]
    s_hi = s.astype(BF16)
    s_lo = (s - s_hi.astype(F32)).astype(BF16)
    w_hi = w.astype(BF16)
    w_lo = (w - w_hi.astype(F32)).astype(BF16)
    o_ref[0] = _dot(s_hi, w_hi) + _dot(s_lo, w_hi) + _dot(s_hi, w_lo) + b_ref[0]


def _modulation(cvec, ada_w, ada_b):
    depth, d, n = ada_w.shape
    rows = cvec.shape[0]
    return pl.pallas_call(
        _mod_kernel,
        grid=(depth, n // MOD_COLS),
        in_specs=[pl.BlockSpec((rows, d), lambda l, j: (0, 0)),
                  pl.BlockSpec((1, d, MOD_COLS), lambda l, j: (l, 0, j)),
                  pl.BlockSpec((1, 1, MOD_COLS), lambda l, j: (l, 0, j))],
        out_specs=pl.BlockSpec((1, rows, MOD_COLS), lambda l, j: (l, 0, j)),
        out_shape=jax.ShapeDtypeStruct((depth, rows, n), F32),
        compiler_params=pltpu.CompilerParams(dimension_semantics=("arbitrary", "arbitrary"),
                                             vmem_limit_bytes=VMEM_LIMIT_BYTES),
        name="adaln_modulation",
    )(cvec, ada_w, ada_b.reshape(depth, 1, n))


def _rev_kernel(x_ref, m_ref, nw_ref, w_ref, hlb_ref, s0_ref,
                ob_ref, q_ref, i_ref, st_out_ref,
                st_ref, qf_ref, zb_ref, if_ref, *, layer):
    d = x_ref.shape[-1]
    j = pl.program_id(1)

    @pl.when(j == 0)
    def _():
        st_ref[...] = s0_ref[0]

    m = m_ref[0]
    h = _rms_mod(x_ref[0], nw_ref[...], m[:, 0:d], m[:, d:2 * d]).astype(BF16)
    q = _silu(_dot(h, w_ref[:, 0:d]))
    qf_ref[...] = q
    q_ref[0] = q.astype(BF16)
    zb_ref[...] = _dot(h, w_ref[:, d:2 * d])
    iv = _dot(h, w_ref[:, 2 * d:3 * d])
    if_ref[...] = iv
    i_ref[0] = iv.astype(BF16)
    lb = _lower_bound_row(hlb_ref[...], layer)
    _hgrn_scan(qf_ref, zb_ref, if_ref, ob_ref.at[0], lb, st_ref, reverse=True)

    @pl.when(j == pl.num_programs(1) - 1)
    def _():
        st_out_ref[0] = st_ref[...]


def _reverse_sweep(x, mod, norm_w, w_r, hlb, s0, layer):
    bsz, length, d = x.shape
    tl = min(TOKEN_TILE, length)
    nt = length // tl
    heads = d // HEAD_DIM
    mod_rows = mod.shape[0]
    rev = lambda b, j: (b, nt - 1 - j, 0)
    const2 = lambda b, j: (0, 0)
    state_spec = pl.BlockSpec((1, heads, HEAD_DIM, HEAD_DIM), lambda b, j: (b, 0, 0, 0))
    return pl.pallas_call(
        functools.partial(_rev_kernel, layer=layer),
        grid=(bsz, nt),
        in_specs=[pl.BlockSpec((1, tl, d), rev),
                  pl.BlockSpec((1, 1, N_MOD * d), lambda b, j: (b % mod_rows, 0, 0)),
                  pl.BlockSpec((1, d), const2),
                  pl.BlockSpec((d, 3 * d), const2),
                  pl.BlockSpec(hlb.shape, const2),
                  state_spec],
        out_specs=[pl.BlockSpec((1, tl, d), rev),
                   pl.BlockSpec((1, tl, d), rev),
                   pl.BlockSpec((1, tl, d), rev),
                   state_spec],
        out_shape=[jax.ShapeDtypeStruct((bsz, length, d), F32),
                   jax.ShapeDtypeStruct((bsz, length, d), BF16),
                   jax.ShapeDtypeStruct((bsz, length, d), BF16),
                   jax.ShapeDtypeStruct((bsz, heads, HEAD_DIM, HEAD_DIM), F32)],
        scratch_shapes=[pltpu.VMEM((heads, HEAD_DIM, HEAD_DIM), F32),
                        pltpu.VMEM((tl, d), F32),
                        pltpu.VMEM((tl, d), F32),
                        pltpu.VMEM((tl, d), F32)],
        compiler_params=pltpu.CompilerParams(dimension_semantics=("arbitrary", "arbitrary"),
                                             vmem_limit_bytes=VMEM_LIMIT_BYTES),
        name="hgrn_reverse_sweep",
    )(x, mod, norm_w, w_r, hlb, s0)


def _fwd_kernel(x_ref, m_ref, nw_ref, w_ref, hlb_ref, s0_ref, q_ref, i_ref, ob_ref,
                lnw_ref, lnb_ref, sw_ref, sb_ref, hnw_ref, wa_ref, wb_ref, wo_ref,
                xo_ref, st_out_ref,
                st_ref, zf_ref, of_ref, *, layer):
    d = x_ref.shape[-1]
    tl = x_ref.shape[1]
    j = pl.program_id(1)

    @pl.when(j == 0)
    def _():
        st_ref[...] = s0_ref[0]

    x = x_ref[0]
    m = m_ref[0]
    h = _rms_mod(x, nw_ref[...], m[:, 0:d], m[:, d:2 * d]).astype(BF16)

    zf_ref[...] = _dot(h, w_ref[:, 0:d])
    lb = _lower_bound_row(hlb_ref[...], layer)
    _hgrn_scan(q_ref.at[0], zf_ref, i_ref.at[0], of_ref, lb, st_ref, reverse=False)

    @pl.when(j == pl.num_programs(1) - 1)
    def _():
        st_out_ref[0] = st_ref[...]

    o = of_ref[...] + ob_ref[0]
    parts = []
    for hh in range(d // HEAD_DIM):
        oh = o[:, hh * HEAD_DIM:(hh + 1) * HEAD_DIM]
        parts.append(oh * lax.rsqrt(jnp.mean(oh * oh, axis=-1, keepdims=True) + RMS_EPS))
    og = _dot(h, w_ref[:, 3 * d:4 * d])
    y_b = (jnp.concatenate(parts, axis=-1) * hnw_ref[...]) * _silu(og)
    t_b = _dot(y_b.astype(BF16), wb_ref[...])

    u = _dot(h, w_ref[:, d:2 * d])
    gv = _gelu_tanh(_dot(h, w_ref[:, 2 * d:3 * d]))
    mu = jnp.mean(gv, axis=-1, keepdims=True)
    cen = gv - mu
    var = jnp.mean(cen * cen, axis=-1, keepdims=True)
    vn = (cen * lax.rsqrt(var + LN_EPS) * lnw_ref[...] + lnb_ref[...]).astype(BF16)
    gd = d // SGU_GROUPS
    row_blocks = []
    for n in range(tl // SGU_CHUNK):
        rs = slice(n * SGU_CHUNK, (n + 1) * SGU_CHUNK)
        cols = [_dot(sw_ref[g], vn[rs, g * gd:(g + 1) * gd]) for g in range(SGU_GROUPS)]
        row_blocks.append(jnp.concatenate(cols, axis=-1) + sb_ref[...])
    mixed = jnp.concatenate(row_blocks, axis=0)
    y_a = _gelu_tanh(u) * mixed
    t_a = _dot(y_a.astype(BF16), wa_ref[...])

    gate_a = _dot(h, w_ref[:, 4 * d:5 * d])
    gate_b = _dot(h, w_ref[:, 5 * d:6 * d])
    merged = _sigmoid(gate_a) * t_a + _sigmoid(gate_b) * t_b
    y = _dot(merged.astype(BF16), wo_ref[...])
    xo_ref[0] = x + m[:, 2 * d:3 * d] * y


def _forward_sweep(x, mod, norm_w, w_f, hlb, s0, q, iv, ob, ln_w, ln_b, sgu_w, sgu_bias,
                   hnorm_w, w_a, w_b, w_o, layer):
    bsz, length, d = x.shape
    tl = min(TOKEN_TILE, length)
    nt = length // tl
    heads = d // HEAD_DIM
    mod_rows = mod.shape[0]
    tok = lambda b, j: (b, j, 0)
    const2 = lambda b, j: (0, 0)
    const3 = lambda b, j: (0, 0, 0)
    once = dict(pipeline_mode=pl.Buffered(1))
    state_spec = pl.BlockSpec((1, heads, HEAD_DIM, HEAD_DIM), lambda b, j: (b, 0, 0, 0))
    return pl.pallas_call(
        functools.partial(_fwd_kernel, layer=layer),
        grid=(bsz, nt),
        in_specs=[pl.BlockSpec((1, tl, d), tok),
                  pl.BlockSpec((1, 1, N_MOD * d), lambda b, j: (b % mod_rows, 0, 0)),
                  pl.BlockSpec((1, d), const2),
                  pl.BlockSpec((d, 6 * d), const2, **once),
                  pl.BlockSpec(hlb.shape, const2),
                  state_spec,
                  pl.BlockSpec((1, tl, d), tok),
                  pl.BlockSpec((1, tl, d), tok),
                  pl.BlockSpec((1, tl, d), tok),
                  pl.BlockSpec((1, d), const2),
                  pl.BlockSpec((1, d), const2),
                  pl.BlockSpec(sgu_w.shape, const3, **once),
                  pl.BlockSpec(sgu_bias.shape, const2, **once),
                  pl.BlockSpec((1, d), const2),
                  pl.BlockSpec((d, d), const2, **once),
                  pl.BlockSpec((d, d), const2, **once),
                  pl.BlockSpec((d, d), const2, **once)],
        out_specs=[pl.BlockSpec((1, tl, d), tok), state_spec],
        out_shape=[jax.ShapeDtypeStruct((bsz, length, d), F32),
                   jax.ShapeDtypeStruct((bsz, heads, HEAD_DIM, HEAD_DIM), F32)],
        scratch_shapes=[pltpu.VMEM((heads, HEAD_DIM, HEAD_DIM), F32),
                        pltpu.VMEM((tl, d), F32),
                        pltpu.VMEM((tl, d), F32)],
        compiler_params=pltpu.CompilerParams(dimension_semantics=("arbitrary", "arbitrary"),
                                             vmem_limit_bytes=VMEM_LIMIT_BYTES),
        name="token_mixer_forward_sweep",
    )(x, mod, norm_w, w_f, hlb, s0, q, iv, ob, ln_w, ln_b, sgu_w, sgu_bias, hnorm_w, w_a, w_b, w_o)


def _ffn_kernel(xp_ref, x_ref, xn_ref, m_ref, nw_ref, wa_ref, wv_ref, cw_ref, cb_ref, wd_ref, fw_ref,
                xo_ref, h_ref, acc_ref, *, on_grid, final_norm):
    d = x_ref.shape[-1]
    tl = x_ref.shape[1]
    halo = xp_ref.shape[1]
    j = pl.program_id(1)
    m = m_ref[0]
    shift, scale, gate = m[:, 3 * d:4 * d], m[:, 4 * d:5 * d], m[:, 5 * d:6 * d]
    x = x_ref[0]
    h_ref[halo:halo + tl, :] = _rms_mod(x, nw_ref[...], shift, scale).astype(BF16)
    h_ref[0:halo, :] = _rms_mod(xp_ref[0], nw_ref[...], shift, scale).astype(BF16)
    h_ref[halo + tl:, :] = _rms_mod(xn_ref[0], nw_ref[...], shift, scale).astype(BF16)

    row = lax.broadcasted_iota(jnp.int32, (tl + 2 * halo, 1), 0)
    if on_grid:
        valid = jnp.logical_and(jnp.logical_or(row >= halo, j > 0),
                                jnp.logical_or(row < halo + tl, j < pl.num_programs(1) - 1))
        period = GRID_W
    else:
        valid = jnp.logical_and(row >= halo, row < halo + tl)
        period = tl
    assert period & (period - 1) == 0
    col = jnp.bitwise_and(lax.broadcasted_iota(jnp.int32, (tl, 1), 0), period - 1)
    has_left = col != 0
    has_right = col != period - 1
    acc_ref[...] = jnp.zeros_like(acc_ref)

    def body(cb, carry):
        a = jnp.where(valid, _dot(h_ref[...], wa_ref[cb]), 0.0)
        v = _dot(h_ref[halo:halo + tl, :], wv_ref[cb])
        cw = cw_ref[cb]
        mid = a[halo:halo + tl]
        if on_grid:
            up = a[halo - GRID_W:halo - GRID_W + tl]
            dn = a[halo + GRID_W:halo + GRID_W + tl]
            p = [cw[dx:dx + 1] * up + cw[3 + dx:4 + dx] * mid + cw[6 + dx:7 + dx] * dn for dx in range(CONV_W)]
        else:
            p = [cw[3 + dx:4 + dx] * mid for dx in range(CONV_W)]
        left = jnp.where(has_left, pltpu.roll(p[0], 1, axis=0), 0.0)
        right = jnp.where(has_right, pltpu.roll(p[2], tl - 1, axis=0), 0.0)
        conv = left + p[1] + right + cb_ref[cb]
        act = (_gelu_tanh(conv) * v).astype(BF16)
        acc_ref[...] += _dot(act, wd_ref[cb])
        return carry

    lax.fori_loop(0, wa_ref.shape[0], body, 0)
    out = x + gate * acc_ref[...]
    if final_norm:
        out = out * lax.rsqrt(jnp.mean(out * out, axis=-1, keepdims=True) + RMS_EPS) * fw_ref[...]
    xo_ref[0] = out


def _conv_ffn(x, mod, norm_w, w_up_a, w_up_v, conv_w, conv_b, w_down, final_w, on_grid, final_norm):
    bsz, length, d = x.shape
    tl = min(FFN_TILE, length)
    nt = length // tl
    halo = GRID_W
    nb, _, cols = w_up_a.shape
    mod_rows = mod.shape[0]
    hb = tl // halo
    n_hb = length // halo
    const2 = lambda b, j: (0, 0)
    const3 = lambda b, j: (0, 0, 0)
    once = dict(pipeline_mode=pl.Buffered(1))
    return pl.pallas_call(
        functools.partial(_ffn_kernel, on_grid=on_grid, final_norm=final_norm),
        grid=(bsz, nt),
        in_specs=[pl.BlockSpec((1, halo, d), lambda b, j: (b, jnp.maximum(j * hb - 1, 0), 0)),
                  pl.BlockSpec((1, tl, d), lambda b, j: (b, j, 0)),
                  pl.BlockSpec((1, halo, d), lambda b, j: (b, jnp.minimum((j + 1) * hb, n_hb - 1), 0)),
                  pl.BlockSpec((1, 1, N_MOD * d), lambda b, j: (b % mod_rows, 0, 0)),
                  pl.BlockSpec((1, d), const2),
                  pl.BlockSpec((nb, d, cols), const3, **once),
                  pl.BlockSpec((nb, d, cols), const3, **once),
                  pl.BlockSpec((nb, CONV_W * CONV_W, cols), const3, **once),
                  pl.BlockSpec((nb, 1, cols), const3, **once),
                  pl.BlockSpec((nb, cols, d), const3, **once),
                  pl.BlockSpec((1, d), const2)],
        out_specs=pl.BlockSpec((1, tl, d), lambda b, j: (b, j, 0)),
        out_shape=jax.ShapeDtypeStruct((bsz, length, d), F32),
        scratch_shapes=[pltpu.VMEM((tl + 2 * halo, d), BF16),
                        pltpu.VMEM((tl, d), F32)],
        compiler_params=pltpu.CompilerParams(dimension_semantics=("arbitrary", "arbitrary"),
                                             vmem_limit_bytes=VMEM_LIMIT_BYTES),
        name="conv_ffn",
    )(x, x, x, mod, norm_w, w_up_a, w_up_v, conv_w, conv_b, w_down, final_w)


def kernel(x, c, ctx, c_ctx, ada_w, ada_b, norm1_w, w_in, sgu_ln_w, sgu_ln_b, sgu_w, sgu_b,
           hgrn_lower_bounds, hgrn_norm_w, w_branch_a, w_branch_b, w_out, norm2_w,
           ffn_w_up, ffn_conv_w, ffn_conv_b, ffn_w_down, final_norm_w):
    depth = ada_w.shape[0]
    bsz, _, d = x.shape
    heads = d // HEAD_DIM
    d_ff = ffn_w_down.shape[1]
    nb = d_ff // FFN_COLS
    assert d_ff % FFN_COLS == 0 and (N_MOD * d) % MOD_COLS == 0

    pad = (-(bsz + 1)) % 8
    cvec = jnp.concatenate([c, c_ctx[None, :], jnp.zeros((pad, d), F32)], axis=0)
    mod = _modulation(cvec, ada_w, ada_b)
    mod_x = mod[:, :bsz, None, :]
    mod_c = mod[:, bsz:bsz + 1, None, :]

    hlb_f = hgrn_lower_bounds[:, :d]
    hlb_b = hgrn_lower_bounds[:, d:]
    zero_state = jnp.zeros((bsz, heads, HEAD_DIM, HEAD_DIM), F32)
    row = lambda t: t[None, :]

    for l in range(depth):
        last = l == depth - 1
        wl = w_in[l].astype(BF16)
        blk = lambda n: wl[:, n * d:(n + 1) * d]
        w_r = jnp.concatenate([blk(0), blk(2), blk(3)], axis=1)
        w_f = jnp.concatenate([blk(1)] + [blk(n) for n in range(4, 9)], axis=1)
        sw = sgu_w[l].astype(BF16)
        sgu_bias = jnp.repeat(sgu_b[l].T, d // SGU_GROUPS, axis=1)
        hnw = jnp.tile(hgrn_norm_w[l], heads)[None, :]
        w_a, w_b, w_o = (t[l].astype(BF16) for t in (w_branch_a, w_branch_b, w_out))
        w_up = ffn_w_up[l].astype(BF16)
        split_cols = lambda t: t.reshape(t.shape[0], nb, FFN_COLS).transpose(1, 0, 2)
        w_up_a, w_up_v = split_cols(w_up[:, :d_ff]), split_cols(w_up[:, d_ff:])
        conv_w = split_cols(ffn_conv_w[l].reshape(CONV_W * CONV_W, d_ff))
        conv_b = split_cols(ffn_conv_b[l][None, :])
        w_down = ffn_w_down[l].astype(BF16).reshape(nb, FFN_COLS, d)

        mixer = functools.partial(_forward_sweep, ln_w=row(sgu_ln_w[l]), ln_b=row(sgu_ln_b[l]), sgu_w=sw,
                                  sgu_bias=sgu_bias, hnorm_w=hnw, w_a=w_a, w_b=w_b, w_o=w_o, layer=l)
        ffn = functools.partial(_conv_ffn, norm_w=row(norm2_w[l]), w_up_a=w_up_a, w_up_v=w_up_v, conv_w=conv_w,
                                conv_b=conv_b, w_down=w_down, final_w=row(final_norm_w))

        ob_c, q_c, i_c, s_b = _reverse_sweep(ctx, mod_c[l], row(norm1_w[l]), w_r, hlb_b, zero_state, l)
        ctx_mixed, s_f = mixer(ctx, mod_c[l], row(norm1_w[l]), w_f, hlb_f, zero_state, q_c, i_c, ob_c)
        ob_x, q_x, i_x, _ = _reverse_sweep(x, mod_x[l], row(norm1_w[l]), w_r, hlb_b, s_b, l)
        x, _ = mixer(x, mod_x[l], row(norm1_w[l]), w_f, hlb_f, s_f, q_x, i_x, ob_x)
        x = ffn(x, mod_x[l], on_grid=True, final_norm=last)
        if not last:
            ctx = ffn(ctx_mixed, mod_c[l], on_grid=False, final_norm=False)
    return x
```

```python
import functools

import jax
import jax.numpy as jnp
from jax import lax
from jax.experimental import pallas as pl
from jax.experimental.pallas import tpu as pltpu

F32 = jnp.float32
BF16 = jnp.bfloat16

N_MOD = 6
RMS_EPS = 1e-6
LN_EPS = 1e-5
HEAD_DIM = 128
HGRN_CHUNK = 64
SGU_CHUNK = 128
SGU_GROUPS = 8
GRID_W = 64
CONV_W = 3
IN_Q, IN_F_FWD, IN_F_BWD, IN_I, IN_U, IN_V, IN_OG, IN_GATE_A, IN_GATE_B = range(9)

TOKEN_TILE = 256
FFN_TILE = 512
FFN_COLS = 256
MOD_COLS = 1536
VMEM_LIMIT_BYTES = 56 * 1024 * 1024


def _dot(a, b):
    return jnp.dot(a, b, preferred_element_type=F32)


def _dot_nt(a, b):
    return lax.dot_general(a, b, (((1,), (1,)), ((), ())), preferred_element_type=F32)


def _sigmoid(x):
    return 1.0 / (1.0 + jnp.exp(-x))


def _silu(x):
    return x * _sigmoid(x)


def _gelu_tanh(x):
    return 0.5 * x * (1.0 + jnp.tanh(0.7978845608028654 * (x + 0.044715 * (x * x * x))))


def _rms_mod(x, norm_w, shift, scale):
    y = x * lax.rsqrt(jnp.mean(x * x, axis=-1, keepdims=True) + RMS_EPS) * norm_w
    return y * (1.0 + scale) + shift


def _lower_bound_row(hlb, layer):
    depth = hlb.shape[0]
    rows = [hlb[j:j + 1] for j in range(depth)]
    mx = functools.reduce(jnp.maximum, rows)
    es = [jnp.exp(r - mx) for r in rows]
    tot = functools.reduce(lambda a, b: a + b, es)
    acc = jnp.zeros_like(rows[0])
    for j in range(1, layer + 1):
        acc = acc + es[j] / tot
    return acc


def _hgrn_tile(q, z, v, lb, st_ref, reverse):
    tl, d = q.shape
    n_chunks = tl // HGRN_CHUNK
    shift = HGRN_CHUNK.bit_length() - 1
    r = lax.broadcasted_iota(jnp.int32, (tl, tl), 0)
    c = lax.broadcasted_iota(jnp.int32, (tl, tl), 1)
    same_chunk = jnp.right_shift(r, shift) == jnp.right_shift(c, shift)
    ordered = (c >= r) if reverse else (c <= r)
    tri = jnp.where(jnp.logical_and(same_chunk, ordered), 1.0, 0.0).astype(BF16)
    mask = ordered[:HGRN_CHUNK, :HGRN_CHUNK]

    e = jnp.exp(-z)
    sig = 1.0 / (1.0 + e)
    f = lb + (1.0 - lb) * sig
    k = (1.0 - lb) * (e * sig)
    g = jnp.log(f)
    g_hi = g.astype(BF16)
    g_lo = (g - g_hi.astype(F32)).astype(BF16)
    b = _dot(tri, g_hi) + _dot(tri, g_lo)
    half = HGRN_CHUNK // 2
    ref_i, last_i = (half, 0) if reverse else (half - 1, HGRN_CHUNK - 1)
    refs = [b[n * HGRN_CHUNK + ref_i:n * HGRN_CHUNK + ref_i + 1] for n in range(n_chunks)]
    lasts = [b[n * HGRN_CHUNK + last_i:n * HGRN_CHUNK + last_i + 1] for n in range(n_chunks)]
    ref_rows = jnp.concatenate([jnp.broadcast_to(t, (HGRN_CHUNK, d)) for t in refs], axis=0)
    qs = (q * jnp.exp(b - ref_rows)).astype(BF16)
    ks = (k * jnp.exp(ref_rows - b)).astype(BF16)
    vb = v.astype(BF16)

    heads = d // HEAD_DIM
    states = [st_ref[h] for h in range(heads)]
    outs = [None] * n_chunks
    for n in (reversed(range(n_chunks)) if reverse else range(n_chunks)):
        rs = slice(n * HGRN_CHUNK, (n + 1) * HGRN_CHUNK)
        e_ref = jnp.exp(refs[n])
        e_last_ref = jnp.exp(lasts[n] - refs[n])
        decay = jnp.exp(lasts[n])
        cols = []
        for h in range(heads):
            sl = slice(h * HEAD_DIM, (h + 1) * HEAD_DIM)
            scores = jnp.where(mask, _dot_nt(qs[rs, sl], ks[rs, sl]), 0.0).astype(BF16)
            inter = _dot_nt(qs[rs, sl], (states[h] * e_ref[:, sl]).astype(BF16))
            cols.append(_dot(scores, vb[rs, sl]) + inter)
            kv_t = _dot(v[rs, sl].T.astype(BF16), ks[rs, sl])
            states[h] = states[h] * decay[:, sl] + kv_t * e_last_ref[:, sl]
        outs[n] = jnp.concatenate(cols, axis=-1)
    for h in range(heads):
        st_ref[h] = states[h]
    return jnp.concatenate(outs, axis=0)


def _mod_kernel(c_ref, w_ref, b_ref, o_ref):
    s = _silu(c_ref[...])
    w = w_ref[0]
    s_hi = s.astype(BF16)
    s_lo = (s - s_hi.astype(F32)).astype(BF16)
    w_hi = w.astype(BF16)
    w_lo = (w - w_hi.astype(F32)).astype(BF16)
    o_ref[0] = _dot(s_hi, w_hi) + _dot(s_lo, w_hi) + _dot(s_hi, w_lo) + b_ref[0]


def _modulation(cvec, ada_w, ada_b):
    depth, d, n = ada_w.shape
    rows = cvec.shape[0]
    return pl.pallas_call(
        _mod_kernel,
        grid=(depth, n // MOD_COLS),
        in_specs=[pl.BlockSpec((rows, d), lambda l, j: (0, 0)),
                  pl.BlockSpec((1, d, MOD_COLS), lambda l, j: (l, 0, j)),
                  pl.BlockSpec((1, 1, MOD_COLS), lambda l, j: (l, 0, j))],
        out_specs=pl.BlockSpec((1, rows, MOD_COLS), lambda l, j: (l, 0, j)),
        out_shape=jax.ShapeDtypeStruct((depth, rows, n), F32),
        compiler_params=pltpu.CompilerParams(dimension_semantics=("arbitrary", "arbitrary"),
                                             vmem_limit_bytes=VMEM_LIMIT_BYTES),
        name="adaln_modulation",
    )(cvec, ada_w, ada_b.reshape(depth, 1, n))


def _rev_kernel(x_ref, m_ref, nw_ref, wq_ref, wfb_ref, wi_ref, hlb_ref, s0_ref,
                ob_ref, q_ref, i_ref, st_out_ref,
                st_ref, *, layer):
    d = x_ref.shape[-1]
    j = pl.program_id(1)

    @pl.when(j == 0)
    def _():
        st_ref[...] = s0_ref[0]

    m = m_ref[0]
    h = _rms_mod(x_ref[0], nw_ref[...], m[:, 0:d], m[:, d:2 * d]).astype(BF16)
    q = _silu(_dot(h, wq_ref[...]))
    q_ref[0] = q.astype(BF16)
    zb = _dot(h, wfb_ref[...])
    iv = _dot(h, wi_ref[...])
    i_ref[0] = iv.astype(BF16)
    lb = _lower_bound_row(hlb_ref[...], layer)
    ob_ref[0] = _hgrn_tile(q, zb, iv, lb, st_ref, reverse=True)

    @pl.when(j == pl.num_programs(1) - 1)
    def _():
        st_out_ref[0] = st_ref[...]


def _reverse_sweep(x, mod, norm_w, w_in, hlb, s0, layer):
    bsz, length, d = x.shape
    tl = min(TOKEN_TILE, length)
    nt = length // tl
    heads = d // HEAD_DIM
    mod_rows = mod.shape[0]
    rev = lambda b, j: (b, nt - 1 - j, 0)
    const2 = lambda b, j: (0, 0)
    w_block = lambda n: pl.BlockSpec((d, d), lambda b, j: (0, n))
    state_spec = pl.BlockSpec((1, heads, HEAD_DIM, HEAD_DIM), lambda b, j: (b, 0, 0, 0))
    return pl.pallas_call(
        functools.partial(_rev_kernel, layer=layer),
        grid=(bsz, nt),
        in_specs=[pl.BlockSpec((1, tl, d), rev),
                  pl.BlockSpec((1, 1, N_MOD * d), lambda b, j: (b % mod_rows, 0, 0)),
                  pl.BlockSpec((1, d), const2),
                  w_block(IN_Q), w_block(IN_F_BWD), w_block(IN_I),
                  pl.BlockSpec(hlb.shape, const2),
                  state_spec],
        out_specs=[pl.BlockSpec((1, tl, d), rev),
                   pl.BlockSpec((1, tl, d), rev),
                   pl.BlockSpec((1, tl, d), rev),
                   state_spec],
        out_shape=[jax.ShapeDtypeStruct((bsz, length, d), F32),
                   jax.ShapeDtypeStruct((bsz, length, d), BF16),
                   jax.ShapeDtypeStruct((bsz, length, d), BF16),
                   jax.ShapeDtypeStruct((bsz, heads, HEAD_DIM, HEAD_DIM), F32)],
        scratch_shapes=[pltpu.VMEM((heads, HEAD_DIM, HEAD_DIM), F32)],
        compiler_params=pltpu.CompilerParams(dimension_semantics=("arbitrary", "arbitrary"),
                                             vmem_limit_bytes=VMEM_LIMIT_BYTES),
        name="hgrn_reverse_sweep",
    )(x, mod, norm_w, w_in, w_in, w_in, hlb, s0)


def _fwd_kernel(x_ref, m_ref, nw_ref, wff_ref, wu_ref, wv_ref, wog_ref, wga_ref, wgb_ref,
                hlb_ref, s0_ref, q_ref, i_ref, ob_ref,
                lnw_ref, lnb_ref, sw_ref, sb_ref, hnw_ref, wa_ref, wb_ref, wo_ref,
                xo_ref, st_out_ref,
                st_ref, *, layer):
    d = x_ref.shape[-1]
    tl = x_ref.shape[1]
    j = pl.program_id(1)

    @pl.when(j == 0)
    def _():
        st_ref[...] = s0_ref[0]

    x = x_ref[0]
    m = m_ref[0]
    h = _rms_mod(x, nw_ref[...], m[:, 0:d], m[:, d:2 * d]).astype(BF16)

    lb = _lower_bound_row(hlb_ref[...], layer)
    o_f = _hgrn_tile(q_ref[0].astype(F32), _dot(h, wff_ref[...]), i_ref[0].astype(F32), lb, st_ref, reverse=False)

    @pl.when(j == pl.num_programs(1) - 1)
    def _():
        st_out_ref[0] = st_ref[...]

    o = o_f + ob_ref[0]
    parts = []
    for hh in range(d // HEAD_DIM):
        oh = o[:, hh * HEAD_DIM:(hh + 1) * HEAD_DIM]
        parts.append(oh * lax.rsqrt(jnp.mean(oh * oh, axis=-1, keepdims=True) + RMS_EPS))
    og = _dot(h, wog_ref[...])
    y_b = (jnp.concatenate(parts, axis=-1) * hnw_ref[...]) * _silu(og)
    t_b = _dot(y_b.astype(BF16), wb_ref[...])

    u = _dot(h, wu_ref[...])
    gv = _gelu_tanh(_dot(h, wv_ref[...]))
    mu = jnp.mean(gv, axis=-1, keepdims=True)
    cen = gv - mu
    var = jnp.mean(cen * cen, axis=-1, keepdims=True)
    vn = (cen * lax.rsqrt(var + LN_EPS) * lnw_ref[...] + lnb_ref[...]).astype(BF16)
    gd = d // SGU_GROUPS
    row_blocks = []
    for n in range(tl // SGU_CHUNK):
        rs = slice(n * SGU_CHUNK, (n + 1) * SGU_CHUNK)
        cols = [_dot(sw_ref[g], vn[rs, g * gd:(g + 1) * gd]) for g in range(SGU_GROUPS)]
        row_blocks.append(jnp.concatenate(cols, axis=-1) + sb_ref[...])
    mixed = jnp.concatenate(row_blocks, axis=0)
    y_a = _gelu_tanh(u) * mixed
    t_a = _dot(y_a.astype(BF16), wa_ref[...])

    gate_a = _dot(h, wga_ref[...])
    gate_b = _dot(h, wgb_ref[...])
    merged = _sigmoid(gate_a) * t_a + _sigmoid(gate_b) * t_b
    y = _dot(merged.astype(BF16), wo_ref[...])
    xo_ref[0] = x + m[:, 2 * d:3 * d] * y


def _forward_sweep(x, mod, norm_w, w_in, hlb, s0, q, iv, ob, ln_w, ln_b, sgu_w, sgu_bias,
                   hnorm_w, w_a, w_b, w_o, layer):
    bsz, length, d = x.shape
    tl = min(TOKEN_TILE, length)
    nt = length // tl
    heads = d // HEAD_DIM
    mod_rows = mod.shape[0]
    tok = lambda b, j: (b, j, 0)
    const2 = lambda b, j: (0, 0)
    const3 = lambda b, j: (0, 0, 0)
    once = dict(pipeline_mode=pl.Buffered(1))
    w_block = lambda n: pl.BlockSpec((d, d), lambda b, j: (0, n), **once)
    state_spec = pl.BlockSpec((1, heads, HEAD_DIM, HEAD_DIM), lambda b, j: (b, 0, 0, 0))
    return pl.pallas_call(
        functools.partial(_fwd_kernel, layer=layer),
        grid=(bsz, nt),
        in_specs=[pl.BlockSpec((1, tl, d), tok),
                  pl.BlockSpec((1, 1, N_MOD * d), lambda b, j: (b % mod_rows, 0, 0)),
                  pl.BlockSpec((1, d), const2),
                  w_block(IN_F_FWD), w_block(IN_U), w_block(IN_V), w_block(IN_OG), w_block(IN_GATE_A),
                  w_block(IN_GATE_B),
                  pl.BlockSpec(hlb.shape, const2),
                  state_spec,
                  pl.BlockSpec((1, tl, d), tok),
                  pl.BlockSpec((1, tl, d), tok),
                  pl.BlockSpec((1, tl, d), tok),
                  pl.BlockSpec((1, d), const2),
                  pl.BlockSpec((1, d), const2),
                  pl.BlockSpec(sgu_w.shape, const3, **once),
                  pl.BlockSpec(sgu_bias.shape, const2, **once),
                  pl.BlockSpec((1, d), const2),
                  pl.BlockSpec((d, d), const2, **once),
                  pl.BlockSpec((d, d), const2, **once),
                  pl.BlockSpec((d, d), const2, **once)],
        out_specs=[pl.BlockSpec((1, tl, d), tok), state_spec],
        out_shape=[jax.ShapeDtypeStruct((bsz, length, d), F32),
                   jax.ShapeDtypeStruct((bsz, heads, HEAD_DIM, HEAD_DIM), F32)],
        scratch_shapes=[pltpu.VMEM((heads, HEAD_DIM, HEAD_DIM), F32)],
        compiler_params=pltpu.CompilerParams(dimension_semantics=("arbitrary", "arbitrary"),
                                             vmem_limit_bytes=VMEM_LIMIT_BYTES),
        name="token_mixer_forward_sweep",
    )(x, mod, norm_w, w_in, w_in, w_in, w_in, w_in, w_in, hlb, s0, q, iv, ob, ln_w, ln_b, sgu_w, sgu_bias,
      hnorm_w, w_a, w_b, w_o)


def _ffn_kernel(xp_ref, x_ref, xn_ref, m_ref, nw_ref, wa_ref, wv_ref, cw_ref, cb_ref, wd_ref, fw_ref,
                xo_ref, h_ref, a_ref, v_ref, act_ref, acc_ref, *, on_grid, final_norm):
    d = x_ref.shape[-1]
    tl = x_ref.shape[1]
    halo = xp_ref.shape[1]
    n_blocks = wa_ref.shape[0]
    j = pl.program_id(1)
    m = m_ref[0]
    shift, scale, gate = m[:, 3 * d:4 * d], m[:, 4 * d:5 * d], m[:, 5 * d:6 * d]
    x = x_ref[0]
    h_ref[halo:halo + tl, :] = _rms_mod(x, nw_ref[...], shift, scale).astype(BF16)
    if on_grid:
        h_prev = _rms_mod(xp_ref[0], nw_ref[...], shift, scale)
        h_next = _rms_mod(xn_ref[0], nw_ref[...], shift, scale)
        h_ref[0:halo, :] = jnp.where(j > 0, h_prev, 0.0).astype(BF16)
        h_ref[halo + tl:, :] = jnp.where(j < pl.num_programs(1) - 1, h_next, 0.0).astype(BF16)
        period = GRID_W
    else:
        h_ref[0:halo, :] = jnp.zeros((halo, d), BF16)
        h_ref[halo + tl:, :] = jnp.zeros((halo, d), BF16)
        period = tl
    assert period & (period - 1) == 0
    col = jnp.bitwise_and(lax.broadcasted_iota(jnp.int32, (tl, 1), 0), period - 1)
    has_left = col != 0
    has_right = col != period - 1
    acc_ref[...] = jnp.zeros_like(acc_ref)

    def up_project(cb, slot):
        a_ref[slot] = _dot(h_ref[...], wa_ref[cb])
        v_ref[slot] = _dot(h_ref[halo:halo + tl, :], wv_ref[cb])

    def conv_gate(cb, slot):
        cw = cw_ref[cb]
        a = a_ref[slot]
        mid = a[halo:halo + tl]
        if on_grid:
            up = a[halo - GRID_W:halo - GRID_W + tl]
            dn = a[halo + GRID_W:halo + GRID_W + tl]
            p = [cw[dx:dx + 1] * up + cw[3 + dx:4 + dx] * mid + cw[6 + dx:7 + dx] * dn for dx in range(CONV_W)]
        else:
            p = [cw[3 + dx:4 + dx] * mid for dx in range(CONV_W)]
        left = jnp.where(has_left, pltpu.roll(p[0], 1, axis=0), 0.0)
        right = jnp.where(has_right, pltpu.roll(p[2], tl - 1, axis=0), 0.0)
        conv = left + p[1] + right + cb_ref[cb]
        act_ref[slot] = _gelu_tanh(conv) * v_ref[slot]

    def down_project(cb, slot):
        acc_ref[...] += _dot(act_ref[slot].astype(BF16), wd_ref[cb])

    def step(i, slot, first=False, second=False, last=False, drain=False):
        if not (last or drain):
            up_project(i, slot)
        if not (first or drain):
            conv_gate(i - 1, 1 - slot)
        if not (first or second):
            down_project(i - 2, slot)

    assert n_blocks % 2 == 1 and n_blocks >= 3
    step(0, 0, first=True)
    step(1, 1, second=True)

    def body(p, carry):
        step(2 * p + 2, 0)
        step(2 * p + 3, 1)
        return carry

    lax.fori_loop(0, (n_blocks - 3) // 2, body, 0)
    step(n_blocks - 1, 0)
    step(n_blocks, 1, last=True)
    step(n_blocks + 1, 0, drain=True)

    out = x + gate * acc_ref[...]
    if final_norm:
        out = out * lax.rsqrt(jnp.mean(out * out, axis=-1, keepdims=True) + RMS_EPS) * fw_ref[...]
    xo_ref[0] = out


def _conv_ffn(x, mod, norm_w, w_up_a, w_up_v, conv_w, conv_b, w_down, final_w, on_grid, final_norm):
    bsz, length, d = x.shape
    tl = min(FFN_TILE, length)
    nt = length // tl
    halo = GRID_W
    nb, _, cols = w_up_a.shape
    mod_rows = mod.shape[0]
    hb = tl // halo
    n_hb = length // halo
    const2 = lambda b, j: (0, 0)
    const3 = lambda b, j: (0, 0, 0)
    once = dict(pipeline_mode=pl.Buffered(1))
    return pl.pallas_call(
        functools.partial(_ffn_kernel, on_grid=on_grid, final_norm=final_norm),
        grid=(bsz, nt),
        in_specs=[pl.BlockSpec((1, halo, d), lambda b, j: (b, jnp.maximum(j * hb - 1, 0), 0)),
                  pl.BlockSpec((1, tl, d), lambda b, j: (b, j, 0)),
                  pl.BlockSpec((1, halo, d), lambda b, j: (b, jnp.minimum((j + 1) * hb, n_hb - 1), 0)),
                  pl.BlockSpec((1, 1, N_MOD * d), lambda b, j: (b % mod_rows, 0, 0)),
                  pl.BlockSpec((1, d), const2),
                  pl.BlockSpec((nb, d, cols), const3, **once),
                  pl.BlockSpec((nb, d, cols), const3, **once),
                  pl.BlockSpec((nb, CONV_W * CONV_W, cols), const3, **once),
                  pl.BlockSpec((nb, 1, cols), const3, **once),
                  pl.BlockSpec((nb, cols, d), const3, **once),
                  pl.BlockSpec((1, d), const2)],
        out_specs=pl.BlockSpec((1, tl, d), lambda b, j: (b, j, 0)),
        out_shape=jax.ShapeDtypeStruct((bsz, length, d), F32),
        scratch_shapes=[pltpu.VMEM((tl + 2 * halo, d), BF16),
                        pltpu.VMEM((2, tl + 2 * halo, cols), F32),
                        pltpu.VMEM((2, tl, cols), F32),
                        pltpu.VMEM((2, tl, cols), F32),
                        pltpu.VMEM((tl, d), F32)],
        compiler_params=pltpu.CompilerParams(dimension_semantics=("arbitrary", "arbitrary"),
                                             vmem_limit_bytes=VMEM_LIMIT_BYTES),
        name="conv_ffn",
    )(x, x, x, mod, norm_w, w_up_a, w_up_v, conv_w, conv_b, w_down, final_w)


def kernel(x, c, ctx, c_ctx, ada_w, ada_b, norm1_w, w_in, sgu_ln_w, sgu_ln_b, sgu_w, sgu_b,
           hgrn_lower_bounds, hgrn_norm_w, w_branch_a, w_branch_b, w_out, norm2_w,
           ffn_w_up, ffn_conv_w, ffn_conv_b, ffn_w_down, final_norm_w):
    depth = ada_w.shape[0]
    bsz, _, d = x.shape
    heads = d // HEAD_DIM
    d_ff = ffn_w_down.shape[1]
    nb = d_ff // FFN_COLS
    assert d_ff % FFN_COLS == 0 and (N_MOD * d) % MOD_COLS == 0

    pad = (-(bsz + 1)) % 8
    cvec = jnp.concatenate([c, c_ctx[None, :], jnp.zeros((pad, d), F32)], axis=0)
    mod = _modulation(cvec, ada_w, ada_b)
    mod_x = mod[:, :bsz, None, :]
    mod_c = mod[:, bsz:bsz + 1, None, :]

    hlb_f = hgrn_lower_bounds[:, :d]
    hlb_b = hgrn_lower_bounds[:, d:]
    zero_state = jnp.zeros((bsz, heads, HEAD_DIM, HEAD_DIM), F32)
    row = lambda t: t[None, :]

    for l in range(depth):
        last = l == depth - 1
        wl = w_in[l].astype(BF16)
        sw = sgu_w[l].astype(BF16)
        sgu_bias = jnp.repeat(sgu_b[l].T, d // SGU_GROUPS, axis=1)
        hnw = jnp.tile(hgrn_norm_w[l], heads)[None, :]
        w_a, w_b, w_o = (t[l].astype(BF16) for t in (w_branch_a, w_branch_b, w_out))
        w_up = ffn_w_up[l].astype(BF16)
        split_cols = lambda t: t.reshape(t.shape[0], nb, FFN_COLS).transpose(1, 0, 2)
        w_up_a, w_up_v = split_cols(w_up[:, :d_ff]), split_cols(w_up[:, d_ff:])
        conv_w = split_cols(ffn_conv_w[l].reshape(CONV_W * CONV_W, d_ff))
        conv_b = split_cols(ffn_conv_b[l][None, :])
        w_down = ffn_w_down[l].astype(BF16).reshape(nb, FFN_COLS, d)

        mixer = functools.partial(_forward_sweep, ln_w=row(sgu_ln_w[l]), ln_b=row(sgu_ln_b[l]), sgu_w=sw,
                                  sgu_bias=sgu_bias, hnorm_w=hnw, w_a=w_a, w_b=w_b, w_o=w_o, layer=l)
        ffn = functools.partial(_conv_ffn, norm_w=row(norm2_w[l]), w_up_a=w_up_a, w_up_v=w_up_v, conv_w=conv_w,
                                conv_b=conv_b, w_down=w_down, final_w=row(final_norm_w))

        ob_c, q_c, i_c, s_b = _reverse_sweep(ctx, mod_c[l], row(norm1_w[l]), wl, hlb_b, zero_state, l)
        ctx_mixed, s_f = mixer(ctx, mod_c[l], row(norm1_w[l]), wl, hlb_f, zero_state, q_c, i_c, ob_c)
        ob_x, q_x, i_x, _ = _reverse_sweep(x, mod_x[l], row(norm1_w[l]), wl, hlb_b, s_b, l)
        x, _ = mixer(x, mod_x[l], row(norm1_w[l]), wl, hlb_f, s_f, q_x, i_x, ob_x)
        x = ffn(x, mod_x[l], on_grid=True, final_norm=last)
        if not last:
            ctx = ffn(ctx_mixed, mod_c[l], on_grid=False, final_norm=False)
    return x
```

```python
import functools

import jax
import jax.numpy as jnp
from jax import lax
from jax.experimental import pallas as pl
from jax.experimental.pallas import tpu as pltpu

F32 = jnp.float32
BF16 = jnp.bfloat16

N_MOD = 6
RMS_EPS = 1e-6
LN_EPS = 1e-5
HEAD_DIM = 128
HGRN_CHUNK = 64
CUMSUM_ROWS = 256
SGU_CHUNK = 128
SGU_GROUPS = 8
GRID_W = 64
CONV_W = 3
IN_Q, IN_F_FWD, IN_F_BWD, IN_I, IN_U, IN_V, IN_OG, IN_GATE_A, IN_GATE_B = range(9)

TOKEN_TILE = 512
FFN_TILE = 512
FFN_COLS = 256
MOD_COLS = 1536
VMEM_LIMIT_BYTES = 56 * 1024 * 1024


def _dot(a, b):
    return jnp.dot(a, b, preferred_element_type=F32)


def _dot_nt(a, b):
    return lax.dot_general(a, b, (((1,), (1,)), ((), ())), preferred_element_type=F32)


def _sigmoid(x):
    return 1.0 / (1.0 + jnp.exp(-x))


def _silu(x):
    return x * _sigmoid(x)


GELU_C0 = 0.7978845608028654
GELU_C1 = GELU_C0 * 0.044715


def _gelu_tanh_gate(x, gate):
    hg = (0.5 * x) * gate
    return hg * jnp.tanh(x * ((x * x) * GELU_C1 + GELU_C0)) + hg


def _gelu_tanh(x):
    hx = 0.5 * x
    return hx * jnp.tanh(x * ((x * x) * GELU_C1 + GELU_C0)) + hx


def _rms_mod(x, norm_w, shift, scale):
    y = x * lax.rsqrt(jnp.mean(x * x, axis=-1, keepdims=True) + RMS_EPS) * norm_w
    return y * (1.0 + scale) + shift


def _lower_bound_row(hlb, layer):
    depth = hlb.shape[0]
    rows = [hlb[j:j + 1] for j in range(depth)]
    mx = functools.reduce(jnp.maximum, rows)
    es = [jnp.exp(r - mx) for r in rows]
    tot = functools.reduce(lambda a, b: a + b, es)
    acc = jnp.zeros_like(rows[0])
    for j in range(1, layer + 1):
        acc = acc + es[j] / tot
    return acc


def _hgrn_tile(q, z, v, lb, st_ref, reverse, need_out=True):
    tl, d = z.shape
    n_chunks = tl // HGRN_CHUNK
    shift = HGRN_CHUNK.bit_length() - 1
    cs = min(tl, CUMSUM_ROWS)
    r = lax.broadcasted_iota(jnp.int32, (cs, cs), 0)
    c = lax.broadcasted_iota(jnp.int32, (cs, cs), 1)
    same_chunk = jnp.right_shift(r, shift) == jnp.right_shift(c, shift)
    ordered = (c >= r) if reverse else (c <= r)
    tri = jnp.where(jnp.logical_and(same_chunk, ordered), 1.0, 0.0).astype(BF16)
    mask = ordered[:HGRN_CHUNK, :HGRN_CHUNK]

    e = jnp.exp(-z)
    sig = 1.0 / (1.0 + e)
    f = lb + (1.0 - lb) * sig
    k = (1.0 - lb) * (e * sig)
    g = jnp.log(f)
    g_hi = g.astype(BF16)
    g_lo = (g - g_hi.astype(F32)).astype(BF16)
    b = jnp.concatenate([_dot(tri, g_hi[i:i + cs]) + _dot(tri, g_lo[i:i + cs]) for i in range(0, tl, cs)], axis=0)
    half = HGRN_CHUNK // 2
    ref_i, last_i = (half, 0) if reverse else (half - 1, HGRN_CHUNK - 1)
    refs = [b[n * HGRN_CHUNK + ref_i:n * HGRN_CHUNK + ref_i + 1] for n in range(n_chunks)]
    lasts = [b[n * HGRN_CHUNK + last_i:n * HGRN_CHUNK + last_i + 1] for n in range(n_chunks)]
    ref_rows = jnp.concatenate([jnp.broadcast_to(t, (HGRN_CHUNK, d)) for t in refs], axis=0)
    qs = (q * jnp.exp(b - ref_rows)).astype(BF16) if need_out else None
    ksf = k * jnp.exp(ref_rows - b)
    ks = ksf.astype(BF16)
    vb = v.astype(BF16)

    heads = d // HEAD_DIM
    states = [st_ref[h] for h in range(heads)]
    outs = [None] * n_chunks
    for n in (reversed(range(n_chunks)) if reverse else range(n_chunks)):
        rs = slice(n * HGRN_CHUNK, (n + 1) * HGRN_CHUNK)
        e_ref = jnp.exp(refs[n]) if need_out else None
        e_last_ref = jnp.exp(lasts[n] - refs[n])
        decay = jnp.exp(lasts[n])
        cols = []
        for h in range(heads):
            sl = slice(h * HEAD_DIM, (h + 1) * HEAD_DIM)
            if need_out:
                scores = jnp.where(mask, _dot(qs[rs, sl], ksf[rs, sl].T.astype(BF16)), 0.0).astype(BF16)
                inter = _dot(qs[rs, sl], (states[h] * e_ref[:, sl]).T.astype(BF16))
                cols.append(_dot(scores, vb[rs, sl]) + inter)
            kv_t = _dot(v[rs, sl].T.astype(BF16), ks[rs, sl])
            states[h] = states[h] * decay[:, sl] + kv_t * e_last_ref[:, sl]
        if need_out:
            outs[n] = jnp.concatenate(cols, axis=-1)
    for h in range(heads):
        st_ref[h] = states[h]
    return jnp.concatenate(outs, axis=0) if need_out else None


def _mod_kernel(c_ref, w_ref, b_ref, o_ref):
    s = _silu(c_ref[...])
    w = w_ref[0]
    s_hi = s.astype(BF16)
    s_lo = (s - s_hi.astype(F32)).astype(BF16)
    w_hi = w.astype(BF16)
    w_lo = (w - w_hi.astype(F32)).astype(BF16)
    o_ref[0] = _dot(s_hi, w_hi) + _dot(s_lo, w_hi) + _dot(s_hi, w_lo) + b_ref[0]


def _modulation(cvec, ada_w, ada_b):
    depth, d, n = ada_w.shape
    rows = cvec.shape[0]
    return pl.pallas_call(
        _mod_kernel,
        grid=(depth, n // MOD_COLS),
        in_specs=[pl.BlockSpec((rows, d), lambda l, j: (0, 0)),
                  pl.BlockSpec((1, d, MOD_COLS), lambda l, j: (l, 0, j)),
                  pl.BlockSpec((1, 1, MOD_COLS), lambda l, j: (l, 0, j))],
        out_specs=pl.BlockSpec((1, rows, MOD_COLS), lambda l, j: (l, 0, j)),
        out_shape=jax.ShapeDtypeStruct((depth, rows, n), F32),
        compiler_params=pltpu.CompilerParams(dimension_semantics=("arbitrary", "arbitrary"),
                                             vmem_limit_bytes=VMEM_LIMIT_BYTES),
        name="adaln_modulation",
    )(cvec, ada_w, ada_b.reshape(depth, 1, n))


def _rev_kernel(x_ref, m_ref, nw_ref, wq_ref, wfb_ref, wi_ref, hlb_ref, s0_ref,
                ob_ref, q_ref, i_ref, st_out_ref,
                st_ref, *, layer):
    d = x_ref.shape[-1]
    j = pl.program_id(1)

    @pl.when(j == 0)
    def _():
        st_ref[...] = s0_ref[0]

    m = m_ref[0]
    h = _rms_mod(x_ref[0], nw_ref[...], m[:, 0:d], m[:, d:2 * d]).astype(BF16)
    q = _silu(_dot(h, wq_ref[...]))
    q_ref[0] = q.astype(BF16)
    zb = _dot(h, wfb_ref[...])
    iv = _dot(h, wi_ref[...])
    i_ref[0] = iv.astype(BF16)
    lb = _lower_bound_row(hlb_ref[...], layer)
    ob_ref[0] = _hgrn_tile(q, zb, iv, lb, st_ref, reverse=True)

    @pl.when(j == pl.num_programs(1) - 1)
    def _():
        st_out_ref[0] = st_ref[...]


def _reverse_sweep(x, mod, norm_w, w_in, hlb, s0, layer):
    bsz, length, d = x.shape
    tl = min(TOKEN_TILE, length)
    nt = length // tl
    heads = d // HEAD_DIM
    mod_rows = mod.shape[0]
    rev = lambda b, j: (b, nt - 1 - j, 0)
    const2 = lambda b, j: (0, 0)
    w_block = lambda n: pl.BlockSpec((d, d), lambda b, j: (0, n))
    state_spec = pl.BlockSpec((1, heads, HEAD_DIM, HEAD_DIM), lambda b, j: (b, 0, 0, 0))
    return pl.pallas_call(
        functools.partial(_rev_kernel, layer=layer),
        grid=(bsz, nt),
        in_specs=[pl.BlockSpec((1, tl, d), rev),
                  pl.BlockSpec((1, 1, N_MOD * d), lambda b, j: (b % mod_rows, 0, 0)),
                  pl.BlockSpec((1, d), const2),
                  w_block(IN_Q), w_block(IN_F_BWD), w_block(IN_I),
                  pl.BlockSpec(hlb.shape, const2),
                  state_spec],
        out_specs=[pl.BlockSpec((1, tl, d), rev),
                   pl.BlockSpec((1, tl, d), rev),
                   pl.BlockSpec((1, tl, d), rev),
                   state_spec],
        out_shape=[jax.ShapeDtypeStruct((bsz, length, d), F32),
                   jax.ShapeDtypeStruct((bsz, length, d), BF16),
                   jax.ShapeDtypeStruct((bsz, length, d), BF16),
                   jax.ShapeDtypeStruct((bsz, heads, HEAD_DIM, HEAD_DIM), F32)],
        scratch_shapes=[pltpu.VMEM((heads, HEAD_DIM, HEAD_DIM), F32)],
        compiler_params=pltpu.CompilerParams(dimension_semantics=("arbitrary", "arbitrary"),
                                             vmem_limit_bytes=VMEM_LIMIT_BYTES),
        name="hgrn_reverse_sweep",
    )(x, mod, norm_w, w_in, w_in, w_in, hlb, s0)


def _state_kernel(x_ref, m_ref, nw_ref, wf_ref, wi_ref, hlb_ref, s0_ref, st_out_ref, st_ref, *, layer, reverse):
    d = x_ref.shape[-1]
    j = pl.program_id(1)

    @pl.when(j == 0)
    def _():
        st_ref[...] = s0_ref[0]

    m = m_ref[0]
    h = _rms_mod(x_ref[0], nw_ref[...], m[:, 0:d], m[:, d:2 * d]).astype(BF16)
    lb = _lower_bound_row(hlb_ref[...], layer)
    _hgrn_tile(None, _dot(h, wf_ref[...]), _dot(h, wi_ref[...]), lb, st_ref, reverse, need_out=False)

    @pl.when(j == pl.num_programs(1) - 1)
    def _():
        st_out_ref[0] = st_ref[...]


def _state_sweep(x, mod, norm_w, w_in, hlb, s0, layer, reverse):
    bsz, length, d = x.shape
    tl = min(TOKEN_TILE, length)
    nt = length // tl
    heads = d // HEAD_DIM
    mod_rows = mod.shape[0]
    tile = (lambda b, j: (b, nt - 1 - j, 0)) if reverse else (lambda b, j: (b, j, 0))
    const2 = lambda b, j: (0, 0)
    w_block = lambda n: pl.BlockSpec((d, d), lambda b, j: (0, n))
    state_spec = pl.BlockSpec((1, heads, HEAD_DIM, HEAD_DIM), lambda b, j: (b, 0, 0, 0))
    return pl.pallas_call(
        functools.partial(_state_kernel, layer=layer, reverse=reverse),
        grid=(bsz, nt),
        in_specs=[pl.BlockSpec((1, tl, d), tile),
                  pl.BlockSpec((1, 1, N_MOD * d), lambda b, j: (b % mod_rows, 0, 0)),
                  pl.BlockSpec((1, d), const2),
                  w_block(IN_F_BWD if reverse else IN_F_FWD), w_block(IN_I),
                  pl.BlockSpec(hlb.shape, const2),
                  state_spec],
        out_specs=state_spec,
        out_shape=jax.ShapeDtypeStruct((bsz, heads, HEAD_DIM, HEAD_DIM), F32),
        scratch_shapes=[pltpu.VMEM((heads, HEAD_DIM, HEAD_DIM), F32)],
        compiler_params=pltpu.CompilerParams(dimension_semantics=("arbitrary", "arbitrary"),
                                             vmem_limit_bytes=VMEM_LIMIT_BYTES),
        name="hgrn_state_sweep",
    )(x, mod, norm_w, w_in, w_in, hlb, s0)


def _fwd_kernel(x_ref, m_ref, nw_ref, wff_ref, wu_ref, wv_ref, wog_ref, wga_ref, wgb_ref,
                hlb_ref, s0_ref, q_ref, i_ref, ob_ref,
                lnw_ref, lnb_ref, sw_ref, sb_ref, hnw_ref, wa_ref, wb_ref, wo_ref,
                xo_ref, st_out_ref,
                st_ref, *, layer):
    d = x_ref.shape[-1]
    tl = x_ref.shape[1]
    j = pl.program_id(1)

    @pl.when(j == 0)
    def _():
        st_ref[...] = s0_ref[0]

    x = x_ref[0]
    m = m_ref[0]
    h = _rms_mod(x, nw_ref[...], m[:, 0:d], m[:, d:2 * d]).astype(BF16)

    lb = _lower_bound_row(hlb_ref[...], layer)
    o_f = _hgrn_tile(q_ref[0].astype(F32), _dot(h, wff_ref[...]), i_ref[0].astype(F32), lb, st_ref, reverse=False)

    @pl.when(j == pl.num_programs(1) - 1)
    def _():
        st_out_ref[0] = st_ref[...]

    o = o_f + ob_ref[0]
    parts = []
    for hh in range(d // HEAD_DIM):
        oh = o[:, hh * HEAD_DIM:(hh + 1) * HEAD_DIM]
        parts.append(oh * lax.rsqrt(jnp.mean(oh * oh, axis=-1, keepdims=True) + RMS_EPS))
    og = _dot(h, wog_ref[...])
    y_b = (jnp.concatenate(parts, axis=-1) * hnw_ref[...]) * _silu(og)
    t_b = _dot(y_b.astype(BF16), wb_ref[...])

    u = _dot(h, wu_ref[...])
    gv = _gelu_tanh(_dot(h, wv_ref[...]))
    mu = jnp.mean(gv, axis=-1, keepdims=True)
    cen = gv - mu
    var = jnp.mean(cen * cen, axis=-1, keepdims=True)
    vn = (cen * lax.rsqrt(var + LN_EPS) * lnw_ref[...] + lnb_ref[...]).astype(BF16)
    gd = d // SGU_GROUPS
    row_blocks = []
    for n in range(tl // SGU_CHUNK):
        rs = slice(n * SGU_CHUNK, (n + 1) * SGU_CHUNK)
        cols = [_dot(sw_ref[g], vn[rs, g * gd:(g + 1) * gd]) for g in range(SGU_GROUPS)]
        row_blocks.append(jnp.concatenate(cols, axis=-1) + sb_ref[...])
    mixed = jnp.concatenate(row_blocks, axis=0)
    y_a = _gelu_tanh_gate(u, mixed)
    t_a = _dot(y_a.astype(BF16), wa_ref[...])

    gate_a = _dot(h, wga_ref[...])
    gate_b = _dot(h, wgb_ref[...])
    merged = _sigmoid(gate_a) * t_a + _sigmoid(gate_b) * t_b
    y = _dot(merged.astype(BF16), wo_ref[...])
    xo_ref[0] = x + m[:, 2 * d:3 * d] * y


def _forward_sweep(x, mod, norm_w, w_in, hlb, s0, q, iv, ob, ln_w, ln_b, sgu_w, sgu_bias,
                   hnorm_w, w_a, w_b, w_o, layer):
    bsz, length, d = x.shape
    tl = min(TOKEN_TILE, length)
    nt = length // tl
    heads = d // HEAD_DIM
    mod_rows = mod.shape[0]
    tok = lambda b, j: (b, j, 0)
    const2 = lambda b, j: (0, 0)
    const3 = lambda b, j: (0, 0, 0)
    once = dict(pipeline_mode=pl.Buffered(1))
    w_block = lambda n: pl.BlockSpec((d, d), lambda b, j: (0, n), **once)
    state_spec = pl.BlockSpec((1, heads, HEAD_DIM, HEAD_DIM), lambda b, j: (b, 0, 0, 0))
    return pl.pallas_call(
        functools.partial(_fwd_kernel, layer=layer),
        grid=(bsz, nt),
        in_specs=[pl.BlockSpec((1, tl, d), tok),
                  pl.BlockSpec((1, 1, N_MOD * d), lambda b, j: (b % mod_rows, 0, 0)),
                  pl.BlockSpec((1, d), const2),
                  w_block(IN_F_FWD), w_block(IN_U), w_block(IN_V), w_block(IN_OG), w_block(IN_GATE_A),
                  w_block(IN_GATE_B),
                  pl.BlockSpec(hlb.shape, const2),
                  state_spec,
                  pl.BlockSpec((1, tl, d), tok),
                  pl.BlockSpec((1, tl, d), tok),
                  pl.BlockSpec((1, tl, d), tok),
                  pl.BlockSpec((1, d), const2),
                  pl.BlockSpec((1, d), const2),
                  pl.BlockSpec(sgu_w.shape, const3, **once),
                  pl.BlockSpec(sgu_bias.shape, const2, **once),
                  pl.BlockSpec((1, d), const2),
                  pl.BlockSpec((d, d), const2, **once),
                  pl.BlockSpec((d, d), const2, **once),
                  pl.BlockSpec((d, d), const2, **once)],
        out_specs=[pl.BlockSpec((1, tl, d), tok), state_spec],
        out_shape=[jax.ShapeDtypeStruct((bsz, length, d), F32),
                   jax.ShapeDtypeStruct((bsz, heads, HEAD_DIM, HEAD_DIM), F32)],
        scratch_shapes=[pltpu.VMEM((heads, HEAD_DIM, HEAD_DIM), F32)],
        compiler_params=pltpu.CompilerParams(dimension_semantics=("arbitrary", "arbitrary"),
                                             vmem_limit_bytes=VMEM_LIMIT_BYTES),
        name="token_mixer_forward_sweep",
    )(x, mod, norm_w, w_in, w_in, w_in, w_in, w_in, w_in, hlb, s0, q, iv, ob, ln_w, ln_b, sgu_w, sgu_bias,
      hnorm_w, w_a, w_b, w_o)


def _ffn_kernel(xp_ref, x_ref, xn_ref, m_ref, nw_ref, wu_ref, cw_ref, cb_ref, wd_ref, fw_ref,
                xo_ref, h_ref, a_ref, v_ref, act_ref, acc_ref, *, on_grid, final_norm):
    d = x_ref.shape[-1]
    tl = x_ref.shape[1]
    halo = xp_ref.shape[1]
    d_ff = wd_ref.shape[0]
    cols = a_ref.shape[-1]
    n_blocks = d_ff // cols
    j = pl.program_id(1)
    m = m_ref[0]
    shift, scale, gate = m[:, 3 * d:4 * d], m[:, 4 * d:5 * d], m[:, 5 * d:6 * d]
    x = x_ref[0]
    h_ref[halo:halo + tl, :] = _rms_mod(x, nw_ref[...], shift, scale).astype(BF16)
    if on_grid:
        h_prev = _rms_mod(xp_ref[0], nw_ref[...], shift, scale)
        h_next = _rms_mod(xn_ref[0], nw_ref[...], shift, scale)
        h_ref[0:halo, :] = jnp.where(j > 0, h_prev, 0.0).astype(BF16)
        h_ref[halo + tl:, :] = jnp.where(j < pl.num_programs(1) - 1, h_next, 0.0).astype(BF16)
        period = GRID_W
    else:
        h_ref[0:halo, :] = jnp.zeros((halo, d), BF16)
        h_ref[halo + tl:, :] = jnp.zeros((halo, d), BF16)
        period = tl
    assert period & (period - 1) == 0
    col = jnp.bitwise_and(lax.broadcasted_iota(jnp.int32, (tl, 1), 0), period - 1)
    has_left = col != 0
    has_right = col != period - 1
    acc_ref[...] = jnp.zeros_like(acc_ref)

    def block(cb, base=0):
        return pl.ds(pl.multiple_of(base + cb * cols, cols), cols)

    def up_project(cb, slot):
        a_ref[slot] = _dot(h_ref[...], wu_ref[:, block(cb)])
        v_ref[slot] = _dot(h_ref[halo:halo + tl, :], wu_ref[:, block(cb, d_ff)])

    def conv_gate(cb, slot):
        cw = cw_ref[:, block(cb)]
        a = a_ref[slot]
        mid = a[halo:halo + tl]
        if on_grid:
            up = a[halo - GRID_W:halo - GRID_W + tl]
            dn = a[halo + GRID_W:halo + GRID_W + tl]
            p = [cw[dx:dx + 1] * up + cw[3 + dx:4 + dx] * mid + cw[6 + dx:7 + dx] * dn for dx in range(CONV_W)]
        else:
            p = [cw[3 + dx:4 + dx] * mid for dx in range(CONV_W)]
        left = jnp.where(has_left, pltpu.roll(p[0], 1, axis=0), 0.0)
        right = jnp.where(has_right, pltpu.roll(p[2], tl - 1, axis=0), 0.0)
        conv = left + p[1] + right + cb_ref[:, block(cb)]
        act_ref[slot] = _gelu_tanh_gate(conv, v_ref[slot])

    def down_project(cb, slot):
        acc_ref[...] += _dot(act_ref[slot].astype(BF16), wd_ref[block(cb), :])

    def step(i, slot, first=False, second=False, last=False, drain=False):
        if not (last or drain):
            up_project(i, slot)
        if not (first or drain):
            conv_gate(i - 1, 1 - slot)
        if not (first or second):
            down_project(i - 2, slot)

    assert n_blocks % 2 == 1 and n_blocks >= 3
    step(0, 0, first=True)
    step(1, 1, second=True)

    def body(p, carry):
        step(2 * p + 2, 0)
        step(2 * p + 3, 1)
        return carry

    lax.fori_loop(0, (n_blocks - 3) // 2, body, 0)
    step(n_blocks - 1, 0)
    step(n_blocks, 1, last=True)
    step(n_blocks + 1, 0, drain=True)

    out = x + gate * acc_ref[...]
    if final_norm:
        out = out * lax.rsqrt(jnp.mean(out * out, axis=-1, keepdims=True) + RMS_EPS) * fw_ref[...]
    xo_ref[0] = out


def _conv_ffn(x, mod, norm_w, w_up, conv_w, conv_b, w_down, final_w, on_grid, final_norm):
    bsz, length, d = x.shape
    tl = min(FFN_TILE, length)
    nt = length // tl
    halo = GRID_W
    d_ff = w_down.shape[0]
    cols = FFN_COLS
    mod_rows = mod.shape[0]
    hb = tl // halo
    n_hb = length // halo
    const2 = lambda b, j: (0, 0)
    once = dict(pipeline_mode=pl.Buffered(1))
    return pl.pallas_call(
        functools.partial(_ffn_kernel, on_grid=on_grid, final_norm=final_norm),
        grid=(bsz, nt),
        in_specs=[pl.BlockSpec((1, halo, d), lambda b, j: (b, jnp.maximum(j * hb - 1, 0), 0)),
                  pl.BlockSpec((1, tl, d), lambda b, j: (b, j, 0)),
                  pl.BlockSpec((1, halo, d), lambda b, j: (b, jnp.minimum((j + 1) * hb, n_hb - 1), 0)),
                  pl.BlockSpec((1, 1, N_MOD * d), lambda b, j: (b % mod_rows, 0, 0)),
                  pl.BlockSpec((1, d), const2),
                  pl.BlockSpec((d, 2 * d_ff), const2, **once),
                  pl.BlockSpec((CONV_W * CONV_W, d_ff), const2, **once),
                  pl.BlockSpec((1, d_ff), const2, **once),
                  pl.BlockSpec((d_ff, d), const2, **once),
                  pl.BlockSpec((1, d), const2)],
        out_specs=pl.BlockSpec((1, tl, d), lambda b, j: (b, j, 0)),
        out_shape=jax.ShapeDtypeStruct((bsz, length, d), F32),
        scratch_shapes=[pltpu.VMEM((tl + 2 * halo, d), BF16),
                        pltpu.VMEM((2, tl + 2 * halo, cols), F32),
                        pltpu.VMEM((2, tl, cols), F32),
                        pltpu.VMEM((2, tl, cols), F32),
                        pltpu.VMEM((tl, d), F32)],
        compiler_params=pltpu.CompilerParams(dimension_semantics=("arbitrary", "arbitrary"),
                                             vmem_limit_bytes=VMEM_LIMIT_BYTES),
        name="conv_ffn",
    )(x, x, x, mod, norm_w, w_up, conv_w, conv_b, w_down, final_w)


def kernel(x, c, ctx, c_ctx, ada_w, ada_b, norm1_w, w_in, sgu_ln_w, sgu_ln_b, sgu_w, sgu_b,
           hgrn_lower_bounds, hgrn_norm_w, w_branch_a, w_branch_b, w_out, norm2_w,
           ffn_w_up, ffn_conv_w, ffn_conv_b, ffn_w_down, final_norm_w):
    depth = ada_w.shape[0]
    bsz, _, d = x.shape
    heads = d // HEAD_DIM
    d_ff = ffn_w_down.shape[1]
    assert d_ff % FFN_COLS == 0 and (N_MOD * d) % MOD_COLS == 0

    pad = (-(bsz + 1)) % 8
    cvec = jnp.concatenate([c, c_ctx[None, :], jnp.zeros((pad, d), F32)], axis=0)
    mod = _modulation(cvec, ada_w, ada_b)
    mod_x = mod[:, :bsz, None, :]
    mod_c = mod[:, bsz:bsz + 1, None, :]

    hlb_f = hgrn_lower_bounds[:, :d]
    hlb_b = hgrn_lower_bounds[:, d:]
    zero_state = jnp.zeros((bsz, heads, HEAD_DIM, HEAD_DIM), F32)
    row = lambda t: t[None, :]

    for l in range(depth):
        last = l == depth - 1
        wl = w_in[l].astype(BF16)
        sw = sgu_w[l].astype(BF16)
        sgu_bias = jnp.repeat(sgu_b[l].T, d // SGU_GROUPS, axis=1)
        hnw = jnp.tile(hgrn_norm_w[l], heads)[None, :]
        w_a, w_b, w_o = (t[l].astype(BF16) for t in (w_branch_a, w_branch_b, w_out))
        w_up = ffn_w_up[l].astype(BF16)
        conv_w = ffn_conv_w[l].reshape(CONV_W * CONV_W, d_ff)
        conv_b = ffn_conv_b[l][None, :]
        w_down = ffn_w_down[l].astype(BF16)

        mixer = functools.partial(_forward_sweep, ln_w=row(sgu_ln_w[l]), ln_b=row(sgu_ln_b[l]), sgu_w=sw,
                                  sgu_bias=sgu_bias, hnorm_w=hnw, w_a=w_a, w_b=w_b, w_o=w_o, layer=l)
        ffn = functools.partial(_conv_ffn, norm_w=row(norm2_w[l]), w_up=w_up, conv_w=conv_w,
                                conv_b=conv_b, w_down=w_down, final_w=row(final_norm_w))

        if last:
            s_b = _state_sweep(ctx, mod_c[l], row(norm1_w[l]), wl, hlb_b, zero_state, l, reverse=True)
            s_f = _state_sweep(ctx, mod_c[l], row(norm1_w[l]), wl, hlb_f, zero_state, l, reverse=False)
        else:
            ob_c, q_c, i_c, s_b = _reverse_sweep(ctx, mod_c[l], row(norm1_w[l]), wl, hlb_b, zero_state, l)
            ctx_mixed, s_f = mixer(ctx, mod_c[l], row(norm1_w[l]), wl, hlb_f, zero_state, q_c, i_c, ob_c)
        ob_x, q_x, i_x, _ = _reverse_sweep(x, mod_x[l], row(norm1_w[l]), wl, hlb_b, s_b, l)
        x, _ = mixer(x, mod_x[l], row(norm1_w[l]), wl, hlb_f, s_f, q_x, i_x, ob_x)
        x = ffn(x, mod_x[l], on_grid=True, final_norm=last)
        if not last:
            ctx = ffn(ctx_mixed, mod_c[l], on_grid=False, final_norm=False)
    return x
```

```python
import functools

import jax
import jax.numpy as jnp
from jax import lax
from jax.experimental import pallas as pl
from jax.experimental.pallas import tpu as pltpu

F32 = jnp.float32
BF16 = jnp.bfloat16

N_MOD = 6
RMS_EPS = 1e-6
LN_EPS = 1e-5
HEAD_DIM = 128
HGRN_CHUNK = 64
CUMSUM_ROWS = 256
SGU_CHUNK = 128
SGU_GROUPS = 8
GRID_W = 64
CONV_W = 3
IN_Q, IN_F_FWD, IN_F_BWD, IN_I, IN_U, IN_V, IN_OG, IN_GATE_A, IN_GATE_B = range(9)

TOKEN_TILE = 512
FFN_TILE = 512
MOD_COLS = 1536
VMEM_LIMIT_BYTES = 56 * 1024 * 1024


def _dot(a, b):
    return jnp.dot(a, b, preferred_element_type=F32)


def _sigmoid(x):
    return 1.0 / (1.0 + jnp.exp(-x))


def _silu(x):
    return x * _sigmoid(x)


GELU_C0 = 0.7978845608028654
GELU_C1 = GELU_C0 * 0.044715


def _gelu_tanh_gate(x, gate):
    hg = (0.5 * x) * gate
    return hg * jnp.tanh(x * ((x * x) * GELU_C1 + GELU_C0)) + hg


def _gelu_tanh(x):
    hx = 0.5 * x
    return hx * jnp.tanh(x * ((x * x) * GELU_C1 + GELU_C0)) + hx


def _rms_mod(x, norm_w, shift, scale):
    y = x * lax.rsqrt(jnp.mean(x * x, axis=-1, keepdims=True) + RMS_EPS) * norm_w
    return y * (1.0 + scale) + shift


def _lower_bound_row(hlb, layer):
    depth = hlb.shape[0]
    rows = [hlb[j:j + 1] for j in range(depth)]
    mx = functools.reduce(jnp.maximum, rows)
    es = [jnp.exp(r - mx) for r in rows]
    tot = functools.reduce(lambda a, b: a + b, es)
    acc = jnp.zeros_like(rows[0])
    for j in range(1, layer + 1):
        acc = acc + es[j] / tot
    return acc


def _hgrn_tile(q, z, v, lb, st_ref, reverse, need_out=True):
    tl, d = z.shape
    n_chunks = tl // HGRN_CHUNK
    shift = HGRN_CHUNK.bit_length() - 1
    cs = min(tl, CUMSUM_ROWS)
    r = lax.broadcasted_iota(jnp.int32, (cs, cs), 0)
    c = lax.broadcasted_iota(jnp.int32, (cs, cs), 1)
    same_chunk = jnp.right_shift(r, shift) == jnp.right_shift(c, shift)
    ordered = (c >= r) if reverse else (c <= r)
    tri = jnp.where(jnp.logical_and(same_chunk, ordered), 1.0, 0.0).astype(BF16)
    mask = ordered[:HGRN_CHUNK, :HGRN_CHUNK]

    e = jnp.exp(-z)
    sig = 1.0 / (1.0 + e)
    f = lb + (1.0 - lb) * sig
    k = (1.0 - lb) * (e * sig)
    g = jnp.log(f)
    g_hi = g.astype(BF16)
    g_lo = (g - g_hi.astype(F32)).astype(BF16)
    b = jnp.concatenate([_dot(tri, g_hi[i:i + cs]) + _dot(tri, g_lo[i:i + cs]) for i in range(0, tl, cs)], axis=0)
    half = HGRN_CHUNK // 2
    ref_i, last_i = (half, 0) if reverse else (half - 1, HGRN_CHUNK - 1)
    refs = [b[n * HGRN_CHUNK + ref_i:n * HGRN_CHUNK + ref_i + 1] for n in range(n_chunks)]
    lasts = [b[n * HGRN_CHUNK + last_i:n * HGRN_CHUNK + last_i + 1] for n in range(n_chunks)]
    ref_rows = jnp.concatenate([jnp.broadcast_to(t, (HGRN_CHUNK, d)) for t in refs], axis=0)
    qs = (q * jnp.exp(b - ref_rows)).astype(BF16) if need_out else None
    ksf = k * jnp.exp(ref_rows - b)
    ks = ksf.astype(BF16)
    vb = v.astype(BF16)

    heads = d // HEAD_DIM
    states = [st_ref[h] for h in range(heads)]
    outs = [None] * n_chunks
    for n in (reversed(range(n_chunks)) if reverse else range(n_chunks)):
        rs = slice(n * HGRN_CHUNK, (n + 1) * HGRN_CHUNK)
        e_ref = jnp.exp(refs[n]) if need_out else None
        e_last_ref = jnp.exp(lasts[n] - refs[n])
        decay = jnp.exp(lasts[n])
        cols = []
        for h in range(heads):
            sl = slice(h * HEAD_DIM, (h + 1) * HEAD_DIM)
            if need_out:
                scores = jnp.where(mask, _dot(qs[rs, sl], ksf[rs, sl].T.astype(BF16)), 0.0).astype(BF16)
                inter = _dot(qs[rs, sl], (states[h] * e_ref[:, sl]).T.astype(BF16))
                cols.append(_dot(scores, vb[rs, sl]) + inter)
            kv_t = _dot(v[rs, sl].T.astype(BF16), ks[rs, sl])
            states[h] = states[h] * decay[:, sl] + kv_t * e_last_ref[:, sl]
        if need_out:
            outs[n] = jnp.concatenate(cols, axis=-1)
    for h in range(heads):
        st_ref[h] = states[h]
    return jnp.concatenate(outs, axis=0) if need_out else None


def _mod_kernel(c_ref, w_ref, b_ref, o_ref):
    s = _silu(c_ref[...])
    w = w_ref[0]
    s_hi = s.astype(BF16)
    s_lo = (s - s_hi.astype(F32)).astype(BF16)
    w_hi = w.astype(BF16)
    w_lo = (w - w_hi.astype(F32)).astype(BF16)
    o_ref[0] = _dot(s_hi, w_hi) + _dot(s_lo, w_hi) + _dot(s_hi, w_lo) + b_ref[0]


def _modulation(cvec, ada_w, ada_b):
    depth, d, n = ada_w.shape
    rows = cvec.shape[0]
    return pl.pallas_call(
        _mod_kernel,
        grid=(depth, n // MOD_COLS),
        in_specs=[pl.BlockSpec((rows, d), lambda l, j: (0, 0)),
                  pl.BlockSpec((1, d, MOD_COLS), lambda l, j: (l, 0, j)),
                  pl.BlockSpec((1, 1, MOD_COLS), lambda l, j: (l, 0, j))],
        out_specs=pl.BlockSpec((1, rows, MOD_COLS), lambda l, j: (l, 0, j)),
        out_shape=jax.ShapeDtypeStruct((depth, rows, n), F32),
        compiler_params=pltpu.CompilerParams(dimension_semantics=("arbitrary", "arbitrary"),
                                             vmem_limit_bytes=VMEM_LIMIT_BYTES),
        name="adaln_modulation",
    )(cvec, ada_w, ada_b.reshape(depth, 1, n))


def _rev_kernel(x_ref, m_ref, nw_ref, wq_ref, wfb_ref, wi_ref, hlb_ref, s0_ref,
                ob_ref, q_ref, i_ref, st_out_ref,
                st_ref, *, layer):
    d = x_ref.shape[-1]
    j = pl.program_id(1)

    @pl.when(j == 0)
    def _():
        st_ref[...] = s0_ref[0]

    m = m_ref[0]
    h = _rms_mod(x_ref[0], nw_ref[...], m[:, 0:d], m[:, d:2 * d]).astype(BF16)
    q = _silu(_dot(h, wq_ref[...]))
    q_ref[0] = q.astype(BF16)
    zb = _dot(h, wfb_ref[...])
    iv = _dot(h, wi_ref[...])
    i_ref[0] = iv.astype(BF16)
    lb = _lower_bound_row(hlb_ref[...], layer)
    ob_ref[0] = _hgrn_tile(q, zb, iv, lb, st_ref, reverse=True)

    @pl.when(j == pl.num_programs(1) - 1)
    def _():
        st_out_ref[0] = st_ref[...]


def _reverse_sweep(x, mod, norm_w, w_in, hlb, s0, layer):
    bsz, length, d = x.shape
    tl = min(TOKEN_TILE, length)
    nt = length // tl
    heads = d // HEAD_DIM
    mod_rows = mod.shape[0]
    rev = lambda b, j: (b, nt - 1 - j, 0)
    const2 = lambda b, j: (0, 0)
    w_block = lambda n: pl.BlockSpec((d, d), lambda b, j: (0, n))
    state_spec = pl.BlockSpec((1, heads, HEAD_DIM, HEAD_DIM), lambda b, j: (b, 0, 0, 0))
    return pl.pallas_call(
        functools.partial(_rev_kernel, layer=layer),
        grid=(bsz, nt),
        in_specs=[pl.BlockSpec((1, tl, d), rev),
                  pl.BlockSpec((1, 1, N_MOD * d), lambda b, j: (b % mod_rows, 0, 0)),
                  pl.BlockSpec((1, d), const2),
                  w_block(IN_Q), w_block(IN_F_BWD), w_block(IN_I),
                  pl.BlockSpec(hlb.shape, const2),
                  state_spec],
        out_specs=[pl.BlockSpec((1, tl, d), rev),
                   pl.BlockSpec((1, tl, d), rev),
                   pl.BlockSpec((1, tl, d), rev),
                   state_spec],
        out_shape=[jax.ShapeDtypeStruct((bsz, length, d), F32),
                   jax.ShapeDtypeStruct((bsz, length, d), BF16),
                   jax.ShapeDtypeStruct((bsz, length, d), BF16),
                   jax.ShapeDtypeStruct((bsz, heads, HEAD_DIM, HEAD_DIM), F32)],
        scratch_shapes=[pltpu.VMEM((heads, HEAD_DIM, HEAD_DIM), F32)],
        compiler_params=pltpu.CompilerParams(dimension_semantics=("arbitrary", "arbitrary"),
                                             vmem_limit_bytes=VMEM_LIMIT_BYTES),
        name="hgrn_reverse_sweep",
    )(x, mod, norm_w, w_in, w_in, w_in, hlb, s0)


def _state_kernel(x_ref, m_ref, nw_ref, wf_ref, wi_ref, hlb_ref, s0_ref, st_out_ref, st_ref, *, layer, reverse):
    d = x_ref.shape[-1]
    j = pl.program_id(1)

    @pl.when(j == 0)
    def _():
        st_ref[...] = s0_ref[0]

    m = m_ref[0]
    h = _rms_mod(x_ref[0], nw_ref[...], m[:, 0:d], m[:, d:2 * d]).astype(BF16)
    lb = _lower_bound_row(hlb_ref[...], layer)
    _hgrn_tile(None, _dot(h, wf_ref[...]), _dot(h, wi_ref[...]), lb, st_ref, reverse, need_out=False)

    @pl.when(j == pl.num_programs(1) - 1)
    def _():
        st_out_ref[0] = st_ref[...]


def _state_sweep(x, mod, norm_w, w_in, hlb, s0, layer, reverse):
    bsz, length, d = x.shape
    tl = min(TOKEN_TILE, length)
    nt = length // tl
    heads = d // HEAD_DIM
    mod_rows = mod.shape[0]
    tile = (lambda b, j: (b, nt - 1 - j, 0)) if reverse else (lambda b, j: (b, j, 0))
    const2 = lambda b, j: (0, 0)
    w_block = lambda n: pl.BlockSpec((d, d), lambda b, j: (0, n))
    state_spec = pl.BlockSpec((1, heads, HEAD_DIM, HEAD_DIM), lambda b, j: (b, 0, 0, 0))
    return pl.pallas_call(
        functools.partial(_state_kernel, layer=layer, reverse=reverse),
        grid=(bsz, nt),
        in_specs=[pl.BlockSpec((1, tl, d), tile),
                  pl.BlockSpec((1, 1, N_MOD * d), lambda b, j: (b % mod_rows, 0, 0)),
                  pl.BlockSpec((1, d), const2),
                  w_block(IN_F_BWD if reverse else IN_F_FWD), w_block(IN_I),
                  pl.BlockSpec(hlb.shape, const2),
                  state_spec],
        out_specs=state_spec,
        out_shape=jax.ShapeDtypeStruct((bsz, heads, HEAD_DIM, HEAD_DIM), F32),
        scratch_shapes=[pltpu.VMEM((heads, HEAD_DIM, HEAD_DIM), F32)],
        compiler_params=pltpu.CompilerParams(dimension_semantics=("arbitrary", "arbitrary"),
                                             vmem_limit_bytes=VMEM_LIMIT_BYTES),
        name="hgrn_state_sweep",
    )(x, mod, norm_w, w_in, w_in, hlb, s0)


def _fwd_kernel(x_ref, m_ref, nw_ref, wff_ref, wu_ref, wv_ref, wog_ref, wga_ref, wgb_ref,
                hlb_ref, s0_ref, q_ref, i_ref, ob_ref,
                lnw_ref, lnb_ref, sw_ref, sb_ref, hnw_ref, wa_ref, wb_ref, wo_ref,
                xo_ref, st_out_ref,
                st_ref, *, layer):
    d = x_ref.shape[-1]
    tl = x_ref.shape[1]
    j = pl.program_id(1)

    @pl.when(j == 0)
    def _():
        st_ref[...] = s0_ref[0]

    x = x_ref[0]
    m = m_ref[0]
    h = _rms_mod(x, nw_ref[...], m[:, 0:d], m[:, d:2 * d]).astype(BF16)

    lb = _lower_bound_row(hlb_ref[...], layer)
    o_f = _hgrn_tile(q_ref[0].astype(F32), _dot(h, wff_ref[...]), i_ref[0].astype(F32), lb, st_ref, reverse=False)

    @pl.when(j == pl.num_programs(1) - 1)
    def _():
        st_out_ref[0] = st_ref[...]

    o = o_f + ob_ref[0]
    parts = []
    for hh in range(d // HEAD_DIM):
        oh = o[:, hh * HEAD_DIM:(hh + 1) * HEAD_DIM]
        parts.append(oh * lax.rsqrt(jnp.mean(oh * oh, axis=-1, keepdims=True) + RMS_EPS))
    og = _dot(h, wog_ref[...])
    y_b = (jnp.concatenate(parts, axis=-1) * hnw_ref[...]) * _silu(og)
    t_b = _dot(y_b.astype(BF16), wb_ref[...])

    u = _dot(h, wu_ref[...])
    gv = _gelu_tanh(_dot(h, wv_ref[...]))
    mu = jnp.mean(gv, axis=-1, keepdims=True)
    cen = gv - mu
    var = jnp.mean(cen * cen, axis=-1, keepdims=True)
    vn = (cen * lax.rsqrt(var + LN_EPS) * lnw_ref[...] + lnb_ref[...]).astype(BF16)
    gd = d // SGU_GROUPS
    row_blocks = []
    for n in range(tl // SGU_CHUNK):
        rs = slice(n * SGU_CHUNK, (n + 1) * SGU_CHUNK)
        cols = [_dot(sw_ref[g], vn[rs, g * gd:(g + 1) * gd]) for g in range(SGU_GROUPS)]
        row_blocks.append(jnp.concatenate(cols, axis=-1) + sb_ref[...])
    mixed = jnp.concatenate(row_blocks, axis=0)
    y_a = _gelu_tanh_gate(u, mixed)
    t_a = _dot(y_a.astype(BF16), wa_ref[...])

    gate_a = _dot(h, wga_ref[...])
    gate_b = _dot(h, wgb_ref[...])
    merged = _sigmoid(gate_a) * t_a + _sigmoid(gate_b) * t_b
    y = _dot(merged.astype(BF16), wo_ref[...])
    xo_ref[0] = x + m[:, 2 * d:3 * d] * y


def _forward_sweep(x, mod, norm_w, w_in, hlb, s0, q, iv, ob, ln_w, ln_b, sgu_w, sgu_bias,
                   hnorm_w, w_a, w_b, w_o, layer):
    bsz, length, d = x.shape
    tl = min(TOKEN_TILE, length)
    nt = length // tl
    heads = d // HEAD_DIM
    mod_rows = mod.shape[0]
    tok = lambda b, j: (b, j, 0)
    const2 = lambda b, j: (0, 0)
    const3 = lambda b, j: (0, 0, 0)
    once = dict(pipeline_mode=pl.Buffered(1))
    w_block = lambda n: pl.BlockSpec((d, d), lambda b, j: (0, n), **once)
    state_spec = pl.BlockSpec((1, heads, HEAD_DIM, HEAD_DIM), lambda b, j: (b, 0, 0, 0))
    return pl.pallas_call(
        functools.partial(_fwd_kernel, layer=layer),
        grid=(bsz, nt),
        in_specs=[pl.BlockSpec((1, tl, d), tok),
                  pl.BlockSpec((1, 1, N_MOD * d), lambda b, j: (b % mod_rows, 0, 0)),
                  pl.BlockSpec((1, d), const2),
                  w_block(IN_F_FWD), w_block(IN_U), w_block(IN_V), w_block(IN_OG), w_block(IN_GATE_A),
                  w_block(IN_GATE_B),
                  pl.BlockSpec(hlb.shape, const2),
                  state_spec,
                  pl.BlockSpec((1, tl, d), tok),
                  pl.BlockSpec((1, tl, d), tok),
                  pl.BlockSpec((1, tl, d), tok),
                  pl.BlockSpec((1, d), const2),
                  pl.BlockSpec((1, d), const2),
                  pl.BlockSpec(sgu_w.shape, const3, **once),
                  pl.BlockSpec(sgu_bias.shape, const2, **once),
                  pl.BlockSpec((1, d), const2),
                  pl.BlockSpec((d, d), const2, **once),
                  pl.BlockSpec((d, d), const2, **once),
                  pl.BlockSpec((d, d), const2, **once)],
        out_specs=[pl.BlockSpec((1, tl, d), tok), state_spec],
        out_shape=[jax.ShapeDtypeStruct((bsz, length, d), F32),
                   jax.ShapeDtypeStruct((bsz, heads, HEAD_DIM, HEAD_DIM), F32)],
        scratch_shapes=[pltpu.VMEM((heads, HEAD_DIM, HEAD_DIM), F32)],
        compiler_params=pltpu.CompilerParams(dimension_semantics=("arbitrary", "arbitrary"),
                                             vmem_limit_bytes=VMEM_LIMIT_BYTES),
        name="token_mixer_forward_sweep",
    )(x, mod, norm_w, w_in, w_in, w_in, w_in, w_in, w_in, hlb, s0, q, iv, ob, ln_w, ln_b, sgu_w, sgu_bias,
      hnorm_w, w_a, w_b, w_o)


def _ffn_kernel(xp_ref, x_ref, xn_ref, m_ref, nw_ref, wu_ref, cw_ref, cb_ref, wd_ref, fw_ref,
                xo_ref, h_ref, *, on_grid, final_norm):
    d = x_ref.shape[-1]
    tl = x_ref.shape[1]
    halo = xp_ref.shape[1]
    d_ff = wd_ref.shape[0]
    j = pl.program_id(1)
    m = m_ref[0]
    shift, scale, gate = m[:, 3 * d:4 * d], m[:, 4 * d:5 * d], m[:, 5 * d:6 * d]
    x = x_ref[0]
    h_ref[halo:halo + tl, :] = _rms_mod(x, nw_ref[...], shift, scale).astype(BF16)
    if on_grid:
        h_prev = _rms_mod(xp_ref[0], nw_ref[...], shift, scale)
        h_next = _rms_mod(xn_ref[0], nw_ref[...], shift, scale)
        h_ref[0:halo, :] = jnp.where(j > 0, h_prev, 0.0).astype(BF16)
        h_ref[halo + tl:, :] = jnp.where(j < pl.num_programs(1) - 1, h_next, 0.0).astype(BF16)
        period = GRID_W
    else:
        h_ref[0:halo, :] = jnp.zeros((halo, d), BF16)
        h_ref[halo + tl:, :] = jnp.zeros((halo, d), BF16)
        period = tl
    assert period & (period - 1) == 0
    col = jnp.bitwise_and(lax.broadcasted_iota(jnp.int32, (tl, 1), 0), period - 1)
    has_left = col != 0
    has_right = col != period - 1

    a = _dot(h_ref[...], wu_ref[:, 0:d_ff])
    v = _dot(h_ref[halo:halo + tl, :], wu_ref[:, d_ff:2 * d_ff])
    cw = cw_ref[...]
    mid = a[halo:halo + tl]
    if on_grid:
        up = a[halo - GRID_W:halo - GRID_W + tl]
        dn = a[halo + GRID_W:halo + GRID_W + tl]
        p = [cw[dx:dx + 1] * up + cw[3 + dx:4 + dx] * mid + cw[6 + dx:7 + dx] * dn for dx in range(CONV_W)]
    else:
        p = [cw[3 + dx:4 + dx] * mid for dx in range(CONV_W)]
    left = jnp.where(has_left, pltpu.roll(p[0], 1, axis=0), 0.0)
    right = jnp.where(has_right, pltpu.roll(p[2], tl - 1, axis=0), 0.0)
    conv = left + p[1] + right + cb_ref[...]
    act = _gelu_tanh_gate(conv, v).astype(BF16)
    out = x + gate * _dot(act, wd_ref[...])
    if final_norm:
        out = out * lax.rsqrt(jnp.mean(out * out, axis=-1, keepdims=True) + RMS_EPS) * fw_ref[...]
    xo_ref[0] = out


def _conv_ffn(x, mod, norm_w, w_up, conv_w, conv_b, w_down, final_w, on_grid, final_norm):
    bsz, length, d = x.shape
    tl = min(FFN_TILE, length)
    nt = length // tl
    halo = GRID_W
    d_ff = w_down.shape[0]
    mod_rows = mod.shape[0]
    hb = tl // halo
    n_hb = length // halo
    const2 = lambda b, j: (0, 0)
    once = dict(pipeline_mode=pl.Buffered(1))
    return pl.pallas_call(
        functools.partial(_ffn_kernel, on_grid=on_grid, final_norm=final_norm),
        grid=(bsz, nt),
        in_specs=[pl.BlockSpec((1, halo, d), lambda b, j: (b, jnp.maximum(j * hb - 1, 0), 0)),
                  pl.BlockSpec((1, tl, d), lambda b, j: (b, j, 0)),
                  pl.BlockSpec((1, halo, d), lambda b, j: (b, jnp.minimum((j + 1) * hb, n_hb - 1), 0)),
                  pl.BlockSpec((1, 1, N_MOD * d), lambda b, j: (b % mod_rows, 0, 0)),
                  pl.BlockSpec((1, d), const2),
                  pl.BlockSpec((d, 2 * d_ff), const2, **once),
                  pl.BlockSpec((CONV_W * CONV_W, d_ff), const2, **once),
                  pl.BlockSpec((1, d_ff), const2, **once),
                  pl.BlockSpec((d_ff, d), const2, **once),
                  pl.BlockSpec((1, d), const2)],
        out_specs=pl.BlockSpec((1, tl, d), lambda b, j: (b, j, 0)),
        out_shape=jax.ShapeDtypeStruct((bsz, length, d), F32),
        scratch_shapes=[pltpu.VMEM((tl + 2 * halo, d), BF16)],
        compiler_params=pltpu.CompilerParams(dimension_semantics=("arbitrary", "arbitrary"),
                                             vmem_limit_bytes=VMEM_LIMIT_BYTES),
        name="conv_ffn",
    )(x, x, x, mod, norm_w, w_up, conv_w, conv_b, w_down, final_w)


def kernel(x, c, ctx, c_ctx, ada_w, ada_b, norm1_w, w_in, sgu_ln_w, sgu_ln_b, sgu_w, sgu_b,
           hgrn_lower_bounds, hgrn_norm_w, w_branch_a, w_branch_b, w_out, norm2_w,
           ffn_w_up, ffn_conv_w, ffn_conv_b, ffn_w_down, final_norm_w):
    depth = ada_w.shape[0]
    bsz, _, d = x.shape
    heads = d // HEAD_DIM
    d_ff = ffn_w_down.shape[1]
    assert (N_MOD * d) % MOD_COLS == 0

    pad = (-(bsz + 1)) % 8
    cvec = jnp.concatenate([c, c_ctx[None, :], jnp.zeros((pad, d), F32)], axis=0)
    mod = _modulation(cvec, ada_w, ada_b)
    mod_x = mod[:, :bsz, None, :]
    mod_c = mod[:, bsz:bsz + 1, None, :]

    hlb_f = hgrn_lower_bounds[:, :d]
    hlb_b = hgrn_lower_bounds[:, d:]
    zero_state = jnp.zeros((bsz, heads, HEAD_DIM, HEAD_DIM), F32)
    row = lambda t: t[None, :]

    for l in range(depth):
        last = l == depth - 1
        wl = w_in[l].astype(BF16)
        sw = sgu_w[l].astype(BF16)
        sgu_bias = jnp.repeat(sgu_b[l].T, d // SGU_GROUPS, axis=1)
        hnw = jnp.tile(hgrn_norm_w[l], heads)[None, :]
        w_a, w_b, w_o = (t[l].astype(BF16) for t in (w_branch_a, w_branch_b, w_out))
        w_up = ffn_w_up[l].astype(BF16)
        conv_w = ffn_conv_w[l].reshape(CONV_W * CONV_W, d_ff)
        conv_b = ffn_conv_b[l][None, :]
        w_down = ffn_w_down[l].astype(BF16)

        mixer = functools.partial(_forward_sweep, ln_w=row(sgu_ln_w[l]), ln_b=row(sgu_ln_b[l]), sgu_w=sw,
                                  sgu_bias=sgu_bias, hnorm_w=hnw, w_a=w_a, w_b=w_b, w_o=w_o, layer=l)
        ffn = functools.partial(_conv_ffn, norm_w=row(norm2_w[l]), w_up=w_up, conv_w=conv_w,
                                conv_b=conv_b, w_down=w_down, final_w=row(final_norm_w))

        if last:
            s_b = _state_sweep(ctx, mod_c[l], row(norm1_w[l]), wl, hlb_b, zero_state, l, reverse=True)
            s_f = _state_sweep(ctx, mod_c[l], row(norm1_w[l]), wl, hlb_f, zero_state, l, reverse=False)
        else:
            ob_c, q_c, i_c, s_b = _reverse_sweep(ctx, mod_c[l], row(norm1_w[l]), wl, hlb_b, zero_state, l)
            ctx_mixed, s_f = mixer(ctx, mod_c[l], row(norm1_w[l]), wl, hlb_f, zero_state, q_c, i_c, ob_c)
        ob_x, q_x, i_x, _ = _reverse_sweep(x, mod_x[l], row(norm1_w[l]), wl, hlb_b, s_b, l)
        x, _ = mixer(x, mod_x[l], row(norm1_w[l]), wl, hlb_f, s_f, q_x, i_x, ob_x)
        x = ffn(x, mod_x[l], on_grid=True, final_norm=last)
        if not last:
            ctx = ffn(ctx_mixed, mod_c[l], on_grid=False, final_norm=False)
    return x
```

```python
import functools

import jax
import jax.numpy as jnp
from jax import lax
from jax.experimental import pallas as pl
from jax.experimental.pallas import tpu as pltpu

F32 = jnp.float32
BF16 = jnp.bfloat16

N_MOD = 6
RMS_EPS = 1e-6
LN_EPS = 1e-5
HEAD_DIM = 128
HGRN_CHUNK = 64
CUMSUM_ROWS = 256
SGU_CHUNK = 128
SGU_GROUPS = 8
GRID_W = 64
CONV_W = 3
IN_Q, IN_F_FWD, IN_F_BWD, IN_I, IN_U, IN_V, IN_OG, IN_GATE_A, IN_GATE_B = range(9)

TOKEN_TILE = 512
FFN_TILE = 512
MOD_COLS = 1536
VMEM_LIMIT_BYTES = 56 * 1024 * 1024


def _dot(a, b):
    return jnp.dot(a, b, preferred_element_type=F32)


def _sigmoid(x):
    return 1.0 / (1.0 + jnp.exp(-x))


def _silu(x):
    return x * _sigmoid(x)


GELU_C0 = 0.7978845608028654
GELU_C1 = GELU_C0 * 0.044715


def _gelu_tanh_gate(x, gate):
    hg = (0.5 * x) * gate
    return hg * jnp.tanh(x * ((x * x) * GELU_C1 + GELU_C0)) + hg


def _gelu_tanh(x):
    hx = 0.5 * x
    return hx * jnp.tanh(x * ((x * x) * GELU_C1 + GELU_C0)) + hx


def _rms_mod(x, norm_w, shift, scale):
    y = x * lax.rsqrt(jnp.mean(x * x, axis=-1, keepdims=True) + RMS_EPS) * norm_w
    return y * (1.0 + scale) + shift


def _lower_bound_row(hlb, layer):
    depth = hlb.shape[0]
    rows = [hlb[j:j + 1] for j in range(depth)]
    mx = functools.reduce(jnp.maximum, rows)
    es = [jnp.exp(r - mx) for r in rows]
    tot = functools.reduce(lambda a, b: a + b, es)
    acc = jnp.zeros_like(rows[0])
    for j in range(1, layer + 1):
        acc = acc + es[j] / tot
    return acc


def _hgrn_tile(q, z, v, lb, st_ref, reverse, need_out=True):
    tl, d = z.shape
    n_chunks = tl // HGRN_CHUNK
    assert n_chunks % 2 == 0
    shift = HGRN_CHUNK.bit_length() - 1
    cs = min(tl, CUMSUM_ROWS)
    r = lax.broadcasted_iota(jnp.int32, (cs, cs), 0)
    c = lax.broadcasted_iota(jnp.int32, (cs, cs), 1)
    same_chunk = jnp.right_shift(r, shift) == jnp.right_shift(c, shift)
    ordered = (c >= r) if reverse else (c <= r)
    tri = jnp.where(jnp.logical_and(same_chunk, ordered), 1.0, 0.0).astype(BF16)
    pair = 2 * HGRN_CHUNK
    pair_mask = jnp.logical_and(same_chunk, ordered)[:pair, :pair]

    e = jnp.exp(-z)
    sig = 1.0 / (1.0 + e)
    f = lb + (1.0 - lb) * sig
    k = 1.0 - f
    g = jnp.log(f)
    g_hi = g.astype(BF16)
    g_lo = (g - g_hi.astype(F32)).astype(BF16)
    b = jnp.concatenate([_dot(tri, g_hi[i:i + cs]) + _dot(tri, g_lo[i:i + cs]) for i in range(0, tl, cs)], axis=0)
    half = HGRN_CHUNK // 2
    ref_i, last_i = (half, 0) if reverse else (half - 1, HGRN_CHUNK - 1)
    refs = [b[n * HGRN_CHUNK + ref_i:n * HGRN_CHUNK + ref_i + 1] for n in range(n_chunks)]
    lasts = [b[n * HGRN_CHUNK + last_i:n * HGRN_CHUNK + last_i + 1] for n in range(n_chunks)]
    ref_rows = jnp.concatenate([jnp.broadcast_to(t, (HGRN_CHUNK, d)) for t in refs], axis=0)
    qs = (q * jnp.exp(b - ref_rows)).astype(BF16) if need_out else None
    ksf = k * jnp.exp(ref_rows - b)
    ks = ksf.astype(BF16)
    vb = v.astype(BF16)

    heads = d // HEAD_DIM
    intra = [None] * n_chunks
    if need_out:
        for n in range(0, n_chunks, 2):
            rs2 = slice(n * HGRN_CHUNK, (n + 2) * HGRN_CHUNK)
            cols2 = []
            for h in range(heads):
                sl = slice(h * HEAD_DIM, (h + 1) * HEAD_DIM)
                sc = _dot(qs[rs2, sl], ksf[rs2, sl].T.astype(BF16))
                cols2.append(_dot(jnp.where(pair_mask, sc, 0.0).astype(BF16), vb[rs2, sl]))
            both = jnp.concatenate(cols2, axis=-1)
            intra[n], intra[n + 1] = both[:HGRN_CHUNK], both[HGRN_CHUNK:]
    states = [st_ref[h] for h in range(heads)]
    outs = [None] * n_chunks
    for n in (reversed(range(n_chunks)) if reverse else range(n_chunks)):
        rs = slice(n * HGRN_CHUNK, (n + 1) * HGRN_CHUNK)
        e_ref = jnp.exp(refs[n]) if need_out else None
        e_last_ref = jnp.exp(lasts[n] - refs[n])
        decay = jnp.exp(lasts[n])
        cols = []
        for h in range(heads):
            sl = slice(h * HEAD_DIM, (h + 1) * HEAD_DIM)
            if need_out:
                cols.append(_dot(qs[rs, sl], (states[h] * e_ref[:, sl]).T.astype(BF16)))
            kv_t = _dot(v[rs, sl].T.astype(BF16), ks[rs, sl])
            states[h] = states[h] * decay[:, sl] + kv_t * e_last_ref[:, sl]
        if need_out:
            outs[n] = jnp.concatenate(cols, axis=-1) + intra[n]
    for h in range(heads):
        st_ref[h] = states[h]
    return jnp.concatenate(outs, axis=0) if need_out else None


def _mod_kernel(c_ref, w_ref, b_ref, o_ref):
    s = _silu(c_ref[...])
    w = w_ref[0]
    s_hi = s.astype(BF16)
    s_lo = (s - s_hi.astype(F32)).astype(BF16)
    w_hi = w.astype(BF16)
    w_lo = (w - w_hi.astype(F32)).astype(BF16)
    o_ref[0] = _dot(s_hi, w_hi) + _dot(s_lo, w_hi) + _dot(s_hi, w_lo) + b_ref[0]


def _modulation(cvec, ada_w, ada_b):
    depth, d, n = ada_w.shape
    rows = cvec.shape[0]
    return pl.pallas_call(
        _mod_kernel,
        grid=(depth, n // MOD_COLS),
        in_specs=[pl.BlockSpec((rows, d), lambda l, j: (0, 0)),
                  pl.BlockSpec((1, d, MOD_COLS), lambda l, j: (l, 0, j)),
                  pl.BlockSpec((1, 1, MOD_COLS), lambda l, j: (l, 0, j))],
        out_specs=pl.BlockSpec((1, rows, MOD_COLS), lambda l, j: (l, 0, j)),
        out_shape=jax.ShapeDtypeStruct((depth, rows, n), F32),
        compiler_params=pltpu.CompilerParams(dimension_semantics=("arbitrary", "arbitrary"),
                                             vmem_limit_bytes=VMEM_LIMIT_BYTES),
        name="adaln_modulation",
    )(cvec, ada_w, ada_b.reshape(depth, 1, n))


def _rev_kernel(x_ref, m_ref, nw_ref, wq_ref, wfb_ref, wi_ref, hlb_ref, s0_ref,
                ob_ref, q_ref, i_ref, st_out_ref,
                st_ref, *, layer):
    d = x_ref.shape[-1]
    j = pl.program_id(1)

    @pl.when(j == 0)
    def _():
        st_ref[...] = s0_ref[0]

    m = m_ref[0]
    h = _rms_mod(x_ref[0], nw_ref[...], m[:, 0:d], m[:, d:2 * d]).astype(BF16)
    q = _silu(_dot(h, wq_ref[...]))
    q_ref[0] = q.astype(BF16)
    zb = _dot(h, wfb_ref[...])
    iv = _dot(h, wi_ref[...])
    i_ref[0] = iv.astype(BF16)
    lb = _lower_bound_row(hlb_ref[...], layer)
    ob_ref[0] = _hgrn_tile(q, zb, iv, lb, st_ref, reverse=True)

    @pl.when(j == pl.num_programs(1) - 1)
    def _():
        st_out_ref[0] = st_ref[...]


def _reverse_sweep(x, mod, norm_w, w_in, hlb, s0, layer):
    bsz, length, d = x.shape
    tl = min(TOKEN_TILE, length)
    nt = length // tl
    heads = d // HEAD_DIM
    mod_rows = mod.shape[0]
    rev = lambda b, j: (b, nt - 1 - j, 0)
    const2 = lambda b, j: (0, 0)
    w_block = lambda n: pl.BlockSpec((d, d), lambda b, j: (0, n))
    state_spec = pl.BlockSpec((1, heads, HEAD_DIM, HEAD_DIM), lambda b, j: (b, 0, 0, 0))
    return pl.pallas_call(
        functools.partial(_rev_kernel, layer=layer),
        grid=(bsz, nt),
        in_specs=[pl.BlockSpec((1, tl, d), rev),
                  pl.BlockSpec((1, 1, N_MOD * d), lambda b, j: (b % mod_rows, 0, 0)),
                  pl.BlockSpec((1, d), const2),
                  w_block(IN_Q), w_block(IN_F_BWD), w_block(IN_I),
                  pl.BlockSpec(hlb.shape, const2),
                  state_spec],
        out_specs=[pl.BlockSpec((1, tl, d), rev),
                   pl.BlockSpec((1, tl, d), rev),
                   pl.BlockSpec((1, tl, d), rev),
                   state_spec],
        out_shape=[jax.ShapeDtypeStruct((bsz, length, d), F32),
                   jax.ShapeDtypeStruct((bsz, length, d), BF16),
                   jax.ShapeDtypeStruct((bsz, length, d), BF16),
                   jax.ShapeDtypeStruct((bsz, heads, HEAD_DIM, HEAD_DIM), F32)],
        scratch_shapes=[pltpu.VMEM((heads, HEAD_DIM, HEAD_DIM), F32)],
        compiler_params=pltpu.CompilerParams(dimension_semantics=("arbitrary", "arbitrary"),
                                             vmem_limit_bytes=VMEM_LIMIT_BYTES),
        name="hgrn_reverse_sweep",
    )(x, mod, norm_w, w_in, w_in, w_in, hlb, s0)


def _state_kernel(x_ref, m_ref, nw_ref, wf_ref, wi_ref, hlb_ref, s0_ref, st_out_ref, st_ref, *, layer, reverse):
    d = x_ref.shape[-1]
    j = pl.program_id(1)

    @pl.when(j == 0)
    def _():
        st_ref[...] = s0_ref[0]

    m = m_ref[0]
    h = _rms_mod(x_ref[0], nw_ref[...], m[:, 0:d], m[:, d:2 * d]).astype(BF16)
    lb = _lower_bound_row(hlb_ref[...], layer)
    _hgrn_tile(None, _dot(h, wf_ref[...]), _dot(h, wi_ref[...]), lb, st_ref, reverse, need_out=False)

    @pl.when(j == pl.num_programs(1) - 1)
    def _():
        st_out_ref[0] = st_ref[...]


def _state_sweep(x, mod, norm_w, w_in, hlb, s0, layer, reverse):
    bsz, length, d = x.shape
    tl = min(TOKEN_TILE, length)
    nt = length // tl
    heads = d // HEAD_DIM
    mod_rows = mod.shape[0]
    tile = (lambda b, j: (b, nt - 1 - j, 0)) if reverse else (lambda b, j: (b, j, 0))
    const2 = lambda b, j: (0, 0)
    w_block = lambda n: pl.BlockSpec((d, d), lambda b, j: (0, n))
    state_spec = pl.BlockSpec((1, heads, HEAD_DIM, HEAD_DIM), lambda b, j: (b, 0, 0, 0))
    return pl.pallas_call(
        functools.partial(_state_kernel, layer=layer, reverse=reverse),
        grid=(bsz, nt),
        in_specs=[pl.BlockSpec((1, tl, d), tile),
                  pl.BlockSpec((1, 1, N_MOD * d), lambda b, j: (b % mod_rows, 0, 0)),
                  pl.BlockSpec((1, d), const2),
                  w_block(IN_F_BWD if reverse else IN_F_FWD), w_block(IN_I),
                  pl.BlockSpec(hlb.shape, const2),
                  state_spec],
        out_specs=state_spec,
        out_shape=jax.ShapeDtypeStruct((bsz, heads, HEAD_DIM, HEAD_DIM), F32),
        scratch_shapes=[pltpu.VMEM((heads, HEAD_DIM, HEAD_DIM), F32)],
        compiler_params=pltpu.CompilerParams(dimension_semantics=("arbitrary", "arbitrary"),
                                             vmem_limit_bytes=VMEM_LIMIT_BYTES),
        name="hgrn_state_sweep",
    )(x, mod, norm_w, w_in, w_in, hlb, s0)


def _fwd_kernel(x_ref, m_ref, nw_ref, wff_ref, wu_ref, wv_ref, wog_ref, wga_ref, wgb_ref,
                hlb_ref, s0_ref, q_ref, i_ref, ob_ref,
                lnw_ref, lnb_ref, sw_ref, sb_ref, hnw_ref, wa_ref, wb_ref, wo_ref,
                xo_ref, st_out_ref,
                st_ref, *, layer):
    d = x_ref.shape[-1]
    tl = x_ref.shape[1]
    j = pl.program_id(1)

    @pl.when(j == 0)
    def _():
        st_ref[...] = s0_ref[0]

    x = x_ref[0]
    m = m_ref[0]
    h = _rms_mod(x, nw_ref[...], m[:, 0:d], m[:, d:2 * d]).astype(BF16)

    lb = _lower_bound_row(hlb_ref[...], layer)
    o_f = _hgrn_tile(q_ref[0].astype(F32), _dot(h, wff_ref[...]), i_ref[0].astype(F32), lb, st_ref, reverse=False)

    o = o_f + ob_ref[0]
    parts = []
    for hh in range(d // HEAD_DIM):
        oh = o[:, hh * HEAD_DIM:(hh + 1) * HEAD_DIM]
        parts.append(oh * lax.rsqrt(jnp.mean(oh * oh, axis=-1, keepdims=True) + RMS_EPS))
    og = _dot(h, wog_ref[...])
    y_b = (jnp.concatenate(parts, axis=-1) * hnw_ref[...]) * _silu(og)
    t_b = _dot(y_b.astype(BF16), wb_ref[...])

    u = _dot(h, wu_ref[...])
    gv = _gelu_tanh(_dot(h, wv_ref[...]))
    mu = jnp.mean(gv, axis=-1, keepdims=True)
    cen = gv - mu
    var = jnp.mean(cen * cen, axis=-1, keepdims=True)
    vn = (cen * lax.rsqrt(var + LN_EPS) * lnw_ref[...] + lnb_ref[...]).astype(BF16)
    gd = d // SGU_GROUPS
    row_blocks = []
    for n in range(0, tl // SGU_CHUNK, 2):
        r0 = slice(n * SGU_CHUNK, (n + 1) * SGU_CHUNK)
        r1 = slice((n + 1) * SGU_CHUNK, (n + 2) * SGU_CHUNK)
        two = [_dot(sw_ref[g], jnp.concatenate([vn[r0, g * gd:(g + 1) * gd], vn[r1, g * gd:(g + 1) * gd]], axis=-1))
               for g in range(SGU_GROUPS)]
        row_blocks.append(jnp.concatenate([t[:, :gd] for t in two], axis=-1) + sb_ref[...])
        row_blocks.append(jnp.concatenate([t[:, gd:] for t in two], axis=-1) + sb_ref[...])
    mixed = jnp.concatenate(row_blocks, axis=0)
    y_a = _gelu_tanh_gate(u, mixed)
    t_a = _dot(y_a.astype(BF16), wa_ref[...])

    gate_a = _dot(h, wga_ref[...])
    gate_b = _dot(h, wgb_ref[...])
    merged = _sigmoid(gate_a) * t_a + _sigmoid(gate_b) * t_b
    y = _dot(merged.astype(BF16), wo_ref[...])
    xo_ref[0] = x + m[:, 2 * d:3 * d] * y

    @pl.when(j == pl.num_programs(1) - 1)
    def _():
        st_out_ref[0] = st_ref[...]


def _forward_sweep(x, mod, norm_w, w_in, hlb, s0, q, iv, ob, ln_w, ln_b, sgu_w, sgu_bias,
                   hnorm_w, w_a, w_b, w_o, layer):
    bsz, length, d = x.shape
    tl = min(TOKEN_TILE, length)
    nt = length // tl
    heads = d // HEAD_DIM
    mod_rows = mod.shape[0]
    tok = lambda b, j: (b, j, 0)
    const2 = lambda b, j: (0, 0)
    const3 = lambda b, j: (0, 0, 0)
    once = dict(pipeline_mode=pl.Buffered(1))
    w_block = lambda n: pl.BlockSpec((d, d), lambda b, j: (0, n), **once)
    state_spec = pl.BlockSpec((1, heads, HEAD_DIM, HEAD_DIM), lambda b, j: (b, 0, 0, 0))
    return pl.pallas_call(
        functools.partial(_fwd_kernel, layer=layer),
        grid=(bsz, nt),
        in_specs=[pl.BlockSpec((1, tl, d), tok),
                  pl.BlockSpec((1, 1, N_MOD * d), lambda b, j: (b % mod_rows, 0, 0)),
                  pl.BlockSpec((1, d), const2),
                  w_block(IN_F_FWD), w_block(IN_U), w_block(IN_V), w_block(IN_OG), w_block(IN_GATE_A),
                  w_block(IN_GATE_B),
                  pl.BlockSpec(hlb.shape, const2),
                  state_spec,
                  pl.BlockSpec((1, tl, d), tok),
                  pl.BlockSpec((1, tl, d), tok),
                  pl.BlockSpec((1, tl, d), tok),
                  pl.BlockSpec((1, d), const2),
                  pl.BlockSpec((1, d), const2),
                  pl.BlockSpec(sgu_w.shape, const3, **once),
                  pl.BlockSpec(sgu_bias.shape, const2, **once),
                  pl.BlockSpec((1, d), const2),
                  pl.BlockSpec((d, d), const2, **once),
                  pl.BlockSpec((d, d), const2, **once),
                  pl.BlockSpec((d, d), const2, **once)],
        out_specs=[pl.BlockSpec((1, tl, d), tok), state_spec],
        out_shape=[jax.ShapeDtypeStruct((bsz, length, d), F32),
                   jax.ShapeDtypeStruct((bsz, heads, HEAD_DIM, HEAD_DIM), F32)],
        scratch_shapes=[pltpu.VMEM((heads, HEAD_DIM, HEAD_DIM), F32)],
        compiler_params=pltpu.CompilerParams(dimension_semantics=("arbitrary", "arbitrary"),
                                             vmem_limit_bytes=VMEM_LIMIT_BYTES),
        name="token_mixer_forward_sweep",
    )(x, mod, norm_w, w_in, w_in, w_in, w_in, w_in, w_in, hlb, s0, q, iv, ob, ln_w, ln_b, sgu_w, sgu_bias,
      hnorm_w, w_a, w_b, w_o)


def _ffn_kernel(xp_ref, x_ref, xn_ref, m_ref, nw_ref, wu_ref, cw_ref, cb_ref, wd_ref, fw_ref,
                xo_ref, h_ref, *, on_grid, final_norm):
    d = x_ref.shape[-1]
    tl = x_ref.shape[1]
    halo = xp_ref.shape[1]
    d_ff = wd_ref.shape[0]
    j = pl.program_id(1)
    m = m_ref[0]
    shift, scale, gate = m[:, 3 * d:4 * d], m[:, 4 * d:5 * d], m[:, 5 * d:6 * d]
    x = x_ref[0]
    h_ref[halo:halo + tl, :] = _rms_mod(x, nw_ref[...], shift, scale).astype(BF16)
    if on_grid:
        h_prev = _rms_mod(xp_ref[0], nw_ref[...], shift, scale)
        h_next = _rms_mod(xn_ref[0], nw_ref[...], shift, scale)
        h_ref[0:halo, :] = jnp.where(j > 0, h_prev, 0.0).astype(BF16)
        h_ref[halo + tl:, :] = jnp.where(j < pl.num_programs(1) - 1, h_next, 0.0).astype(BF16)
        period = GRID_W
    else:
        h_ref[0:halo, :] = jnp.zeros((halo, d), BF16)
        h_ref[halo + tl:, :] = jnp.zeros((halo, d), BF16)
        period = tl
    assert period & (period - 1) == 0
    col = jnp.bitwise_and(lax.broadcasted_iota(jnp.int32, (tl, 1), 0), period - 1)
    has_left = col != 0
    has_right = col != period - 1

    a = _dot(h_ref[...], wu_ref[:, 0:d_ff])
    v = _dot(h_ref[halo:halo + tl, :], wu_ref[:, d_ff:2 * d_ff])
    cw = cw_ref[...]
    mid = a[halo:halo + tl]
    if on_grid:
        up = a[halo - GRID_W:halo - GRID_W + tl]
        dn = a[halo + GRID_W:halo + GRID_W + tl]
        p = [cw[dx:dx + 1] * up + cw[3 + dx:4 + dx] * mid + cw[6 + dx:7 + dx] * dn for dx in range(CONV_W)]
    else:
        p = [cw[3 + dx:4 + dx] * mid for dx in range(CONV_W)]
    left = jnp.where(has_left, pltpu.roll(p[0], 1, axis=0), 0.0)
    right = jnp.where(has_right, pltpu.roll(p[2], tl - 1, axis=0), 0.0)
    conv = left + p[1] + right + cb_ref[...]
    act = _gelu_tanh_gate(conv, v).astype(BF16)
    out = x + gate * _dot(act, wd_ref[...])
    if final_norm:
        out = out * lax.rsqrt(jnp.mean(out * out, axis=-1, keepdims=True) + RMS_EPS) * fw_ref[...]
    xo_ref[0] = out


def _conv_ffn(x, mod, norm_w, w_up, conv_w, conv_b, w_down, final_w, on_grid, final_norm):
    bsz, length, d = x.shape
    tl = min(FFN_TILE, length)
    nt = length // tl
    halo = GRID_W
    d_ff = w_down.shape[0]
    mod_rows = mod.shape[0]
    hb = tl // halo
    n_hb = length // halo
    const2 = lambda b, j: (0, 0)
    once = dict(pipeline_mode=pl.Buffered(1))
    return pl.pallas_call(
        functools.partial(_ffn_kernel, on_grid=on_grid, final_norm=final_norm),
        grid=(bsz, nt),
        in_specs=[pl.BlockSpec((1, halo, d), lambda b, j: (b, jnp.maximum(j * hb - 1, 0), 0)),
                  pl.BlockSpec((1, tl, d), lambda b, j: (b, j, 0)),
                  pl.BlockSpec((1, halo, d), lambda b, j: (b, jnp.minimum((j + 1) * hb, n_hb - 1), 0)),
                  pl.BlockSpec((1, 1, N_MOD * d), lambda b, j: (b % mod_rows, 0, 0)),
                  pl.BlockSpec((1, d), const2),
                  pl.BlockSpec((d, 2 * d_ff), const2, **once),
                  pl.BlockSpec((CONV_W * CONV_W, d_ff), const2, **once),
                  pl.BlockSpec((1, d_ff), const2, **once),
                  pl.BlockSpec((d_ff, d), const2, **once),
                  pl.BlockSpec((1, d), const2)],
        out_specs=pl.BlockSpec((1, tl, d), lambda b, j: (b, j, 0)),
        out_shape=jax.ShapeDtypeStruct((bsz, length, d), F32),
        scratch_shapes=[pltpu.VMEM((tl + 2 * halo, d), BF16)],
        compiler_params=pltpu.CompilerParams(dimension_semantics=("arbitrary", "arbitrary"),
                                             vmem_limit_bytes=VMEM_LIMIT_BYTES),
        name="conv_ffn",
    )(x, x, x, mod, norm_w, w_up, conv_w, conv_b, w_down, final_w)


def kernel(x, c, ctx, c_ctx, ada_w, ada_b, norm1_w, w_in, sgu_ln_w, sgu_ln_b, sgu_w, sgu_b,
           hgrn_lower_bounds, hgrn_norm_w, w_branch_a, w_branch_b, w_out, norm2_w,
           ffn_w_up, ffn_conv_w, ffn_conv_b, ffn_w_down, final_norm_w):
    depth = ada_w.shape[0]
    bsz, _, d = x.shape
    heads = d // HEAD_DIM
    d_ff = ffn_w_down.shape[1]
    assert (N_MOD * d) % MOD_COLS == 0

    pad = (-(bsz + 1)) % 8
    cvec = jnp.concatenate([c, c_ctx[None, :], jnp.zeros((pad, d), F32)], axis=0)
    mod = _modulation(cvec, ada_w, ada_b)
    mod_x = mod[:, :bsz, None, :]
    mod_c = mod[:, bsz:bsz + 1, None, :]

    hlb_f = hgrn_lower_bounds[:, :d]
    hlb_b = hgrn_lower_bounds[:, d:]
    zero_state = jnp.zeros((bsz, heads, HEAD_DIM, HEAD_DIM), F32)
    row = lambda t: t[None, :]

    for l in range(depth):
        last = l == depth - 1
        wl = w_in[l].astype(BF16)
        sw = sgu_w[l].astype(BF16)
        sgu_bias = jnp.repeat(sgu_b[l].T, d // SGU_GROUPS, axis=1)
        hnw = jnp.tile(hgrn_norm_w[l], heads)[None, :]
        w_a, w_b, w_o = (t[l].astype(BF16) for t in (w_branch_a, w_branch_b, w_out))
        w_up = ffn_w_up[l].astype(BF16)
        conv_w = ffn_conv_w[l].reshape(CONV_W * CONV_W, d_ff)
        conv_b = ffn_conv_b[l][None, :]
        w_down = ffn_w_down[l].astype(BF16)

        mixer = functools.partial(_forward_sweep, ln_w=row(sgu_ln_w[l]), ln_b=row(sgu_ln_b[l]), sgu_w=sw,
                                  sgu_bias=sgu_bias, hnorm_w=hnw, w_a=w_a, w_b=w_b, w_o=w_o, layer=l)
        ffn = functools.partial(_conv_ffn, norm_w=row(norm2_w[l]), w_up=w_up, conv_w=conv_w,
                                conv_b=conv_b, w_down=w_down, final_w=row(final_norm_w))

        if last:
            s_b = _state_sweep(ctx, mod_c[l], row(norm1_w[l]), wl, hlb_b, zero_state, l, reverse=True)
            s_f = _state_sweep(ctx, mod_c[l], row(norm1_w[l]), wl, hlb_f, zero_state, l, reverse=False)
        else:
            ob_c, q_c, i_c, s_b = _reverse_sweep(ctx, mod_c[l], row(norm1_w[l]), wl, hlb_b, zero_state, l)
            ctx_mixed, s_f = mixer(ctx, mod_c[l], row(norm1_w[l]), wl, hlb_f, zero_state, q_c, i_c, ob_c)
        ob_x, q_x, i_x, _ = _reverse_sweep(x, mod_x[l], row(norm1_w[l]), wl, hlb_b, s_b, l)
        x, _ = mixer(x, mod_x[l], row(norm1_w[l]), wl, hlb_f, s_f, q_x, i_x, ob_x)
        x = ffn(x, mod_x[l], on_grid=True, final_norm=last)
        if not last:
            ctx = ffn(ctx_mixed, mod_c[l], on_grid=False, final_norm=False)
    return x
```

```python
import functools

import jax
import jax.numpy as jnp
from jax import lax
from jax.experimental import pallas as pl
from jax.experimental.pallas import tpu as pltpu

F32 = jnp.float32
BF16 = jnp.bfloat16

N_MOD = 6
RMS_EPS = 1e-6
LN_EPS = 1e-5
HEAD_DIM = 128
HGRN_CHUNK = 64
CUMSUM_ROWS = 256
SGU_CHUNK = 128
SGU_GROUPS = 8
GRID_W = 64
CONV_W = 3
IN_Q, IN_F_FWD, IN_F_BWD, IN_I, IN_U, IN_V, IN_OG, IN_GATE_A, IN_GATE_B = range(9)

TOKEN_TILE = 512
FFN_TILE = 512
MOD_COLS = 1536
VMEM_LIMIT_BYTES = 56 * 1024 * 1024


def _dot(a, b):
    return jnp.dot(a, b, preferred_element_type=F32)


def _sigmoid(x):
    return 1.0 / (1.0 + jnp.exp(-x))


def _silu(x):
    return x * _sigmoid(x)


GELU_C0 = 0.7978845608028654
GELU_C1 = GELU_C0 * 0.044715


def _gelu_tanh_gate(x, gate):
    hg = (0.5 * x) * gate
    return hg * jnp.tanh(x * ((x * x) * GELU_C1 + GELU_C0)) + hg


def _gelu_tanh(x):
    hx = 0.5 * x
    return hx * jnp.tanh(x * ((x * x) * GELU_C1 + GELU_C0)) + hx


def _rms_mod(x, norm_w, shift, scale):
    y = x * lax.rsqrt(jnp.mean(x * x, axis=-1, keepdims=True) + RMS_EPS) * norm_w
    return y * (1.0 + scale) + shift


def _lower_bound_row(hlb, layer):
    depth = hlb.shape[0]
    rows = [hlb[j:j + 1] for j in range(depth)]
    mx = functools.reduce(jnp.maximum, rows)
    es = [jnp.exp(r - mx) for r in rows]
    tot = functools.reduce(lambda a, b: a + b, es)
    acc = jnp.zeros_like(rows[0])
    for j in range(1, layer + 1):
        acc = acc + es[j] / tot
    return acc


def _hgrn_tile(q, z, v, lb, st_ref, reverse, need_out=True):
    tl, d = z.shape
    n_chunks = tl // HGRN_CHUNK
    assert n_chunks % 2 == 0
    shift = HGRN_CHUNK.bit_length() - 1
    cs = min(tl, CUMSUM_ROWS)
    r = lax.broadcasted_iota(jnp.int32, (cs, cs), 0)
    c = lax.broadcasted_iota(jnp.int32, (cs, cs), 1)
    same_chunk = jnp.right_shift(r, shift) == jnp.right_shift(c, shift)
    ordered = (c >= r) if reverse else (c <= r)
    tri = jnp.where(jnp.logical_and(same_chunk, ordered), 1.0, 0.0).astype(BF16)
    pair = 2 * HGRN_CHUNK
    pair_mask = jnp.logical_and(same_chunk, ordered)[:pair, :pair]

    e = jnp.exp(-z)
    sig = 1.0 / (1.0 + e)
    f = lb + (1.0 - lb) * sig
    k = 1.0 - f
    g = jnp.log(f)
    g_hi = g.astype(BF16)
    g_lo = (g - g_hi.astype(F32)).astype(BF16)
    b = jnp.concatenate([_dot(tri, g_hi[i:i + cs]) + _dot(tri, g_lo[i:i + cs]) for i in range(0, tl, cs)], axis=0)
    half = HGRN_CHUNK // 2
    ref_i, last_i = (half, 0) if reverse else (half - 1, HGRN_CHUNK - 1)
    refs = [b[n * HGRN_CHUNK + ref_i:n * HGRN_CHUNK + ref_i + 1] for n in range(n_chunks)]
    lasts = [b[n * HGRN_CHUNK + last_i:n * HGRN_CHUNK + last_i + 1] for n in range(n_chunks)]
    ref_rows = jnp.concatenate([jnp.broadcast_to(t, (HGRN_CHUNK, d)) for t in refs], axis=0)
    qs = (q * jnp.exp(b - ref_rows)).astype(BF16) if need_out else None
    ksf = k * jnp.exp(ref_rows - b)
    ks = ksf.astype(BF16)
    vb = v.astype(BF16)

    heads = d // HEAD_DIM
    intra = [None] * n_chunks
    if need_out:
        for n in range(0, n_chunks, 2):
            rs2 = slice(n * HGRN_CHUNK, (n + 2) * HGRN_CHUNK)
            cols2 = []
            for h in range(heads):
                sl = slice(h * HEAD_DIM, (h + 1) * HEAD_DIM)
                sc = _dot(qs[rs2, sl], ksf[rs2, sl].T.astype(BF16))
                cols2.append(_dot(jnp.where(pair_mask, sc, 0.0).astype(BF16), vb[rs2, sl]))
            both = jnp.concatenate(cols2, axis=-1)
            intra[n], intra[n + 1] = both[:HGRN_CHUNK], both[HGRN_CHUNK:]
    states = [st_ref[h] for h in range(heads)]
    outs = [None] * n_chunks
    for n in (reversed(range(n_chunks)) if reverse else range(n_chunks)):
        rs = slice(n * HGRN_CHUNK, (n + 1) * HGRN_CHUNK)
        e_ref = jnp.exp(refs[n]) if need_out else None
        e_last_ref = jnp.exp(lasts[n] - refs[n])
        decay = jnp.exp(lasts[n])
        cols = []
        for h in range(heads):
            sl = slice(h * HEAD_DIM, (h + 1) * HEAD_DIM)
            if need_out:
                cols.append(_dot(qs[rs, sl], (states[h] * e_ref[:, sl]).T.astype(BF16)))
            kv_t = _dot(v[rs, sl].T.astype(BF16), ks[rs, sl])
            states[h] = states[h] * decay[:, sl] + kv_t * e_last_ref[:, sl]
        if need_out:
            outs[n] = jnp.concatenate(cols, axis=-1) + intra[n]
    for h in range(heads):
        st_ref[h] = states[h]
    return jnp.concatenate(outs, axis=0) if need_out else None


def _mod_kernel(c_ref, w_ref, b_ref, o_ref):
    s = _silu(c_ref[...])
    w = w_ref[0]
    s_hi = s.astype(BF16)
    s_lo = (s - s_hi.astype(F32)).astype(BF16)
    w_hi = w.astype(BF16)
    w_lo = (w - w_hi.astype(F32)).astype(BF16)
    o_ref[0] = _dot(s_hi, w_hi) + _dot(s_lo, w_hi) + _dot(s_hi, w_lo) + b_ref[0]


def _modulation(cvec, ada_w, ada_b):
    depth, d, n = ada_w.shape
    rows = cvec.shape[0]
    return pl.pallas_call(
        _mod_kernel,
        grid=(depth, n // MOD_COLS),
        in_specs=[pl.BlockSpec((rows, d), lambda l, j: (0, 0)),
                  pl.BlockSpec((1, d, MOD_COLS), lambda l, j: (l, 0, j)),
                  pl.BlockSpec((1, 1, MOD_COLS), lambda l, j: (l, 0, j))],
        out_specs=pl.BlockSpec((1, rows, MOD_COLS), lambda l, j: (l, 0, j)),
        out_shape=jax.ShapeDtypeStruct((depth, rows, n), F32),
        compiler_params=pltpu.CompilerParams(dimension_semantics=("arbitrary", "arbitrary"),
                                             vmem_limit_bytes=VMEM_LIMIT_BYTES),
        name="adaln_modulation",
    )(cvec, ada_w, ada_b.reshape(depth, 1, n))


def _rev_kernel(x_ref, m_ref, nw_ref, wq_ref, wfb_ref, wi_ref, hlb_ref, s0_ref,
                ob_ref, q_ref, i_ref, st_out_ref,
                st_ref, *, layer):
    d = x_ref.shape[-1]
    j = pl.program_id(1)

    @pl.when(j == 0)
    def _():
        st_ref[...] = s0_ref[0]

    m = m_ref[0]
    h = _rms_mod(x_ref[0], nw_ref[...], m[:, 0:d], m[:, d:2 * d]).astype(BF16)
    q = _silu(_dot(h, wq_ref[...]))
    q_ref[0] = q.astype(BF16)
    zb = _dot(h, wfb_ref[...])
    iv = _dot(h, wi_ref[...])
    i_ref[0] = iv.astype(BF16)
    lb = _lower_bound_row(hlb_ref[...], layer)
    ob_ref[0] = _hgrn_tile(q, zb, iv, lb, st_ref, reverse=True)

    @pl.when(j == pl.num_programs(1) - 1)
    def _():
        st_out_ref[0] = st_ref[...]


def _reverse_sweep(x, mod, norm_w, w_in, hlb, s0, layer):
    bsz, length, d = x.shape
    tl = min(TOKEN_TILE, length)
    nt = length // tl
    heads = d // HEAD_DIM
    mod_rows = mod.shape[0]
    rev = lambda b, j: (b, nt - 1 - j, 0)
    const2 = lambda b, j: (0, 0)
    w_block = lambda n: pl.BlockSpec((d, d), lambda b, j: (0, n))
    state_spec = pl.BlockSpec((1, heads, HEAD_DIM, HEAD_DIM), lambda b, j: (b, 0, 0, 0))
    return pl.pallas_call(
        functools.partial(_rev_kernel, layer=layer),
        grid=(bsz, nt),
        in_specs=[pl.BlockSpec((1, tl, d), rev),
                  pl.BlockSpec((1, 1, N_MOD * d), lambda b, j: (b % mod_rows, 0, 0)),
                  pl.BlockSpec((1, d), const2),
                  w_block(IN_Q), w_block(IN_F_BWD), w_block(IN_I),
                  pl.BlockSpec(hlb.shape, const2),
                  state_spec],
        out_specs=[pl.BlockSpec((1, tl, d), rev),
                   pl.BlockSpec((1, tl, d), rev),
                   pl.BlockSpec((1, tl, d), rev),
                   state_spec],
        out_shape=[jax.ShapeDtypeStruct((bsz, length, d), F32),
                   jax.ShapeDtypeStruct((bsz, length, d), BF16),
                   jax.ShapeDtypeStruct((bsz, length, d), BF16),
                   jax.ShapeDtypeStruct((bsz, heads, HEAD_DIM, HEAD_DIM), F32)],
        scratch_shapes=[pltpu.VMEM((heads, HEAD_DIM, HEAD_DIM), F32)],
        compiler_params=pltpu.CompilerParams(dimension_semantics=("arbitrary", "arbitrary"),
                                             vmem_limit_bytes=VMEM_LIMIT_BYTES),
        name="hgrn_reverse_sweep",
    )(x, mod, norm_w, w_in, w_in, w_in, hlb, s0)


def _state_kernel(x_ref, m_ref, nw_ref, wf_ref, wi_ref, hlb_ref, s0_ref, st_out_ref, st_ref, *, layer, reverse):
    d = x_ref.shape[-1]
    j = pl.program_id(1)

    @pl.when(j == 0)
    def _():
        st_ref[...] = s0_ref[0]

    m = m_ref[0]
    h = _rms_mod(x_ref[0], nw_ref[...], m[:, 0:d], m[:, d:2 * d]).astype(BF16)
    lb = _lower_bound_row(hlb_ref[...], layer)
    _hgrn_tile(None, _dot(h, wf_ref[...]), _dot(h, wi_ref[...]), lb, st_ref, reverse, need_out=False)

    @pl.when(j == pl.num_programs(1) - 1)
    def _():
        st_out_ref[0] = st_ref[...]


def _state_sweep(x, mod, norm_w, w_in, hlb, s0, layer, reverse):
    bsz, length, d = x.shape
    tl = min(TOKEN_TILE, length)
    nt = length // tl
    heads = d // HEAD_DIM
    mod_rows = mod.shape[0]
    tile = (lambda b, j: (b, nt - 1 - j, 0)) if reverse else (lambda b, j: (b, j, 0))
    const2 = lambda b, j: (0, 0)
    w_block = lambda n: pl.BlockSpec((d, d), lambda b, j: (0, n))
    state_spec = pl.BlockSpec((1, heads, HEAD_DIM, HEAD_DIM), lambda b, j: (b, 0, 0, 0))
    return pl.pallas_call(
        functools.partial(_state_kernel, layer=layer, reverse=reverse),
        grid=(bsz, nt),
        in_specs=[pl.BlockSpec((1, tl, d), tile),
                  pl.BlockSpec((1, 1, N_MOD * d), lambda b, j: (b % mod_rows, 0, 0)),
                  pl.BlockSpec((1, d), const2),
                  w_block(IN_F_BWD if reverse else IN_F_FWD), w_block(IN_I),
                  pl.BlockSpec(hlb.shape, const2),
                  state_spec],
        out_specs=state_spec,
        out_shape=jax.ShapeDtypeStruct((bsz, heads, HEAD_DIM, HEAD_DIM), F32),
        scratch_shapes=[pltpu.VMEM((heads, HEAD_DIM, HEAD_DIM), F32)],
        compiler_params=pltpu.CompilerParams(dimension_semantics=("arbitrary", "arbitrary"),
                                             vmem_limit_bytes=VMEM_LIMIT_BYTES),
        name="hgrn_state_sweep",
    )(x, mod, norm_w, w_in, w_in, hlb, s0)


def _fwd_kernel(x_ref, m_ref, nw_ref, wff_ref, wu_ref, wv_ref, wog_ref, wga_ref, wgb_ref,
                hlb_ref, s0_ref, q_ref, i_ref, ob_ref,
                lnw_ref, lnb_ref, sw_ref, sb_ref, hnw_ref, wa_ref, wb_ref, wo_ref,
                xo_ref, st_out_ref,
                st_ref, *, layer):
    d = x_ref.shape[-1]
    tl = x_ref.shape[1]
    j = pl.program_id(1)

    @pl.when(j == 0)
    def _():
        st_ref[...] = s0_ref[0]

    x = x_ref[0]
    m = m_ref[0]
    h = _rms_mod(x, nw_ref[...], m[:, 0:d], m[:, d:2 * d]).astype(BF16)

    lb = _lower_bound_row(hlb_ref[...], layer)
    o_f = _hgrn_tile(q_ref[0].astype(F32), _dot(h, wff_ref[...]), i_ref[0].astype(F32), lb, st_ref, reverse=False)

    o = o_f + ob_ref[0]
    parts = []
    for hh in range(d // HEAD_DIM):
        oh = o[:, hh * HEAD_DIM:(hh + 1) * HEAD_DIM]
        parts.append(oh * lax.rsqrt(jnp.mean(oh * oh, axis=-1, keepdims=True) + RMS_EPS))
    og = _dot(h, wog_ref[...])
    y_b = (jnp.concatenate(parts, axis=-1) * hnw_ref[...]) * _silu(og)
    t_b = _dot(y_b.astype(BF16), wb_ref[...])

    u = _dot(h, wu_ref[...])
    gv = _gelu_tanh(_dot(h, wv_ref[...]))
    mu = jnp.mean(gv, axis=-1, keepdims=True)
    cen = gv - mu
    var = jnp.mean(cen * cen, axis=-1, keepdims=True)
    vn = (cen * lax.rsqrt(var + LN_EPS) * lnw_ref[...] + lnb_ref[...]).astype(BF16)
    gd = d // SGU_GROUPS
    row_blocks = []
    for n in range(0, tl // SGU_CHUNK, 2):
        r0 = slice(n * SGU_CHUNK, (n + 1) * SGU_CHUNK)
        r1 = slice((n + 1) * SGU_CHUNK, (n + 2) * SGU_CHUNK)
        two = [_dot(sw_ref[g], jnp.concatenate([vn[r0, g * gd:(g + 1) * gd], vn[r1, g * gd:(g + 1) * gd]], axis=-1))
               for g in range(SGU_GROUPS)]
        row_blocks.append(jnp.concatenate([t[:, :gd] for t in two], axis=-1) + sb_ref[...])
        row_blocks.append(jnp.concatenate([t[:, gd:] for t in two], axis=-1) + sb_ref[...])
    mixed = jnp.concatenate(row_blocks, axis=0)
    y_a = _gelu_tanh_gate(u, mixed)
    t_a = _dot(y_a.astype(BF16), wa_ref[...])

    gate_a = _dot(h, wga_ref[...])
    gate_b = _dot(h, wgb_ref[...])
    merged = _sigmoid(gate_a) * t_a + _sigmoid(gate_b) * t_b
    y = _dot(merged.astype(BF16), wo_ref[...])
    xo_ref[0] = x + m[:, 2 * d:3 * d] * y

    @pl.when(j == pl.num_programs(1) - 1)
    def _():
        st_out_ref[0] = st_ref[...]


def _forward_sweep(x, mod, norm_w, w_in, hlb, s0, q, iv, ob, ln_w, ln_b, sgu_w, sgu_bias,
                   hnorm_w, w_a, w_b, w_o, layer):
    bsz, length, d = x.shape
    tl = min(TOKEN_TILE, length)
    nt = length // tl
    heads = d // HEAD_DIM
    mod_rows = mod.shape[0]
    tok = lambda b, j: (b, j, 0)
    const2 = lambda b, j: (0, 0)
    const3 = lambda b, j: (0, 0, 0)
    once = dict(pipeline_mode=pl.Buffered(1))
    w_block = lambda n: pl.BlockSpec((d, d), lambda b, j: (0, n), **once)
    state_spec = pl.BlockSpec((1, heads, HEAD_DIM, HEAD_DIM), lambda b, j: (b, 0, 0, 0))
    return pl.pallas_call(
        functools.partial(_fwd_kernel, layer=layer),
        grid=(bsz, nt),
        in_specs=[pl.BlockSpec((1, tl, d), tok),
                  pl.BlockSpec((1, 1, N_MOD * d), lambda b, j: (b % mod_rows, 0, 0)),
                  pl.BlockSpec((1, d), const2),
                  w_block(IN_F_FWD), w_block(IN_U), w_block(IN_V), w_block(IN_OG), w_block(IN_GATE_A),
                  w_block(IN_GATE_B),
                  pl.BlockSpec(hlb.shape, const2),
                  state_spec,
                  pl.BlockSpec((1, tl, d), tok),
                  pl.BlockSpec((1, tl, d), tok),
                  pl.BlockSpec((1, tl, d), tok),
                  pl.BlockSpec((1, d), const2),
                  pl.BlockSpec((1, d), const2),
                  pl.BlockSpec(sgu_w.shape, const3, **once),
                  pl.BlockSpec(sgu_bias.shape, const2, **once),
                  pl.BlockSpec((1, d), const2),
                  pl.BlockSpec((d, d), const2, **once),
                  pl.BlockSpec((d, d), const2, **once),
                  pl.BlockSpec((d, d), const2, **once)],
        out_specs=[pl.BlockSpec((1, tl, d), tok), state_spec],
        out_shape=[jax.ShapeDtypeStruct((bsz, length, d), F32),
                   jax.ShapeDtypeStruct((bsz, heads, HEAD_DIM, HEAD_DIM), F32)],
        scratch_shapes=[pltpu.VMEM((heads, HEAD_DIM, HEAD_DIM), F32)],
        compiler_params=pltpu.CompilerParams(dimension_semantics=("arbitrary", "arbitrary"),
                                             vmem_limit_bytes=VMEM_LIMIT_BYTES),
        name="token_mixer_forward_sweep",
    )(x, mod, norm_w, w_in, w_in, w_in, w_in, w_in, w_in, hlb, s0, q, iv, ob, ln_w, ln_b, sgu_w, sgu_bias,
      hnorm_w, w_a, w_b, w_o)


def _ffn_kernel(xp_ref, x_ref, xn_ref, m_ref, nw_ref, wu_ref, cw_ref, cb_ref, wd_ref, fw_ref,
                xo_ref, h_ref, *, on_grid, final_norm):
    d = x_ref.shape[-1]
    tl = x_ref.shape[1]
    halo = xp_ref.shape[1]
    d_ff = wd_ref.shape[0]
    j = pl.program_id(1)
    m = m_ref[0]
    shift, scale, gate = m[:, 3 * d:4 * d], m[:, 4 * d:5 * d], m[:, 5 * d:6 * d]
    x = x_ref[0]
    h_ref[halo:halo + tl, :] = _rms_mod(x, nw_ref[...], shift, scale).astype(BF16)
    if on_grid:
        h_prev = _rms_mod(xp_ref[0], nw_ref[...], shift, scale)
        h_next = _rms_mod(xn_ref[0], nw_ref[...], shift, scale)
        h_ref[0:halo, :] = jnp.where(j > 0, h_prev, 0.0).astype(BF16)
        h_ref[halo + tl:, :] = jnp.where(j < pl.num_programs(1) - 1, h_next, 0.0).astype(BF16)
        period = GRID_W
    else:
        period = tl
    assert period & (period - 1) == 0
    col = jnp.bitwise_and(lax.broadcasted_iota(jnp.int32, (tl, 1), 0), period - 1)
    has_left = col != 0
    has_right = col != period - 1

    h_main = h_ref[halo:halo + tl, :]
    v = _dot(h_main, wu_ref[:, d_ff:2 * d_ff].astype(BF16))
    cw = cw_ref[...]
    if on_grid:
        a = _dot(h_ref[...], wu_ref[:, 0:d_ff].astype(BF16))
        mid = a[halo:halo + tl]
        up = a[halo - GRID_W:halo - GRID_W + tl]
        dn = a[halo + GRID_W:halo + GRID_W + tl]
        p = [cw[dx:dx + 1] * up + cw[3 + dx:4 + dx] * mid + cw[6 + dx:7 + dx] * dn for dx in range(CONV_W)]
    else:
        mid = _dot(h_main, wu_ref[:, 0:d_ff].astype(BF16))
        p = [cw[3 + dx:4 + dx] * mid for dx in range(CONV_W)]
    left = jnp.where(has_left, pltpu.roll(p[0], 1, axis=0), 0.0)
    right = jnp.where(has_right, pltpu.roll(p[2], tl - 1, axis=0), 0.0)
    conv = left + p[1] + right + cb_ref[...]
    act = _gelu_tanh_gate(conv, v).astype(BF16)
    out = x + gate * _dot(act, wd_ref[...].astype(BF16))
    if final_norm:
        out = out * lax.rsqrt(jnp.mean(out * out, axis=-1, keepdims=True) + RMS_EPS) * fw_ref[...]
    xo_ref[0] = out


def _conv_ffn(x, mod, norm_w, w_up, conv_w, conv_b, w_down, final_w, layer, on_grid, final_norm):
    bsz, length, d = x.shape
    tl = min(FFN_TILE, length)
    nt = length // tl
    assert on_grid or nt == 1, "the sequence form of the conv keeps the whole sequence in one tile"
    halo = GRID_W
    d_ff = w_down.shape[1]
    mod_rows = mod.shape[0]
    hb = tl // halo
    n_hb = length // halo
    const2 = lambda b, j: (0, 0)
    once = dict(pipeline_mode=pl.Buffered(1))
    return pl.pallas_call(
        functools.partial(_ffn_kernel, on_grid=on_grid, final_norm=final_norm),
        grid=(bsz, nt),
        in_specs=[pl.BlockSpec((1, halo, d), lambda b, j: (b, jnp.maximum(j * hb - 1, 0), 0)),
                  pl.BlockSpec((1, tl, d), lambda b, j: (b, j, 0)),
                  pl.BlockSpec((1, halo, d), lambda b, j: (b, jnp.minimum((j + 1) * hb, n_hb - 1), 0)),
                  pl.BlockSpec((1, 1, N_MOD * d), lambda b, j: (b % mod_rows, 0, 0)),
                  pl.BlockSpec((1, d), const2),
                  pl.BlockSpec((None, d, 2 * d_ff), lambda b, j: (layer, 0, 0), **once),
                  pl.BlockSpec((CONV_W * CONV_W, d_ff), const2, **once),
                  pl.BlockSpec((1, d_ff), const2, **once),
                  pl.BlockSpec((None, d_ff, d), lambda b, j: (layer, 0, 0), **once),
                  pl.BlockSpec((1, d), const2)],
        out_specs=pl.BlockSpec((1, tl, d), lambda b, j: (b, j, 0)),
        out_shape=jax.ShapeDtypeStruct((bsz, length, d), F32),
        scratch_shapes=[pltpu.VMEM((tl + 2 * halo, d), BF16)],
        compiler_params=pltpu.CompilerParams(dimension_semantics=("arbitrary", "arbitrary"),
                                             vmem_limit_bytes=VMEM_LIMIT_BYTES),
        name="conv_ffn",
    )(x, x, x, mod, norm_w, w_up, conv_w, conv_b, w_down, final_w)


def kernel(x, c, ctx, c_ctx, ada_w, ada_b, norm1_w, w_in, sgu_ln_w, sgu_ln_b, sgu_w, sgu_b,
           hgrn_lower_bounds, hgrn_norm_w, w_branch_a, w_branch_b, w_out, norm2_w,
           ffn_w_up, ffn_conv_w, ffn_conv_b, ffn_w_down, final_norm_w):
    depth = ada_w.shape[0]
    bsz, _, d = x.shape
    heads = d // HEAD_DIM
    d_ff = ffn_w_down.shape[1]
    assert (N_MOD * d) % MOD_COLS == 0

    pad = (-(bsz + 1)) % 8
    cvec = jnp.concatenate([c, c_ctx[None, :], jnp.zeros((pad, d), F32)], axis=0)
    mod = _modulation(cvec, ada_w, ada_b)
    mod_x = mod[:, :bsz, None, :]
    mod_c = mod[:, bsz:bsz + 1, None, :]

    hlb_f = hgrn_lower_bounds[:, :d]
    hlb_b = hgrn_lower_bounds[:, d:]
    zero_state = jnp.zeros((bsz, heads, HEAD_DIM, HEAD_DIM), F32)
    row = lambda t: t[None, :]

    for l in range(depth):
        last = l == depth - 1
        wl = w_in[l].astype(BF16)
        sw = sgu_w[l].astype(BF16)
        sgu_bias = jnp.repeat(sgu_b[l].T, d // SGU_GROUPS, axis=1)
        hnw = jnp.tile(hgrn_norm_w[l], heads)[None, :]
        w_a, w_b, w_o = (t[l].astype(BF16) for t in (w_branch_a, w_branch_b, w_out))
        conv_w = ffn_conv_w[l].reshape(CONV_W * CONV_W, d_ff)
        conv_b = ffn_conv_b[l][None, :]

        mixer = functools.partial(_forward_sweep, ln_w=row(sgu_ln_w[l]), ln_b=row(sgu_ln_b[l]), sgu_w=sw,
                                  sgu_bias=sgu_bias, hnorm_w=hnw, w_a=w_a, w_b=w_b, w_o=w_o, layer=l)
        ffn = functools.partial(_conv_ffn, norm_w=row(norm2_w[l]), w_up=ffn_w_up, conv_w=conv_w,
                                conv_b=conv_b, w_down=ffn_w_down, final_w=row(final_norm_w), layer=l)

        if last:
            s_b = _state_sweep(ctx, mod_c[l], row(norm1_w[l]), wl, hlb_b, zero_state, l, reverse=True)
            s_f = _state_sweep(ctx, mod_c[l], row(norm1_w[l]), wl, hlb_f, zero_state, l, reverse=False)
        else:
            ob_c, q_c, i_c, s_b = _reverse_sweep(ctx, mod_c[l], row(norm1_w[l]), wl, hlb_b, zero_state, l)
            ctx_mixed, s_f = mixer(ctx, mod_c[l], row(norm1_w[l]), wl, hlb_f, zero_state, q_c, i_c, ob_c)
        ob_x, q_x, i_x, _ = _reverse_sweep(x, mod_x[l], row(norm1_w[l]), wl, hlb_b, s_b, l)
        x, _ = mixer(x, mod_x[l], row(norm1_w[l]), wl, hlb_f, s_f, q_x, i_x, ob_x)
        x = ffn(x, mod_x[l], on_grid=True, final_norm=last)
        if not last:
            ctx = ffn(ctx_mixed, mod_c[l], on_grid=False, final_norm=False)
    return x
```

```python
import functools

import jax
import jax.numpy as jnp
from jax import lax
from jax.experimental import pallas as pl
from jax.experimental.pallas import tpu as pltpu

F32 = jnp.float32
BF16 = jnp.bfloat16

N_MOD = 6
RMS_EPS = 1e-6
LN_EPS = 1e-5
HEAD_DIM = 128
HGRN_CHUNK = 64
CUMSUM_ROWS = 256
SGU_CHUNK = 128
SGU_GROUPS = 8
GRID_W = 64
CONV_W = 3
IN_Q, IN_F_FWD, IN_F_BWD, IN_I, IN_U, IN_V, IN_OG, IN_GATE_A, IN_GATE_B = range(9)

TOKEN_TILE = 512
FFN_TILE = 512
MOD_COLS = 1536
VMEM_LIMIT_BYTES = 56 * 1024 * 1024


def _dot(a, b):
    return jnp.dot(a, b, preferred_element_type=F32)


def _sigmoid(x):
    return 1.0 / (1.0 + jnp.exp(-x))


def _silu(x):
    return x * _sigmoid(x)


GELU_C0 = 0.7978845608028654
GELU_C1 = GELU_C0 * 0.044715


def _gelu_tanh_gate(x, gate):
    hg = (0.5 * x) * gate
    return hg * jnp.tanh(x * ((x * x) * GELU_C1 + GELU_C0)) + hg


def _gelu_tanh(x):
    hx = 0.5 * x
    return hx * jnp.tanh(x * ((x * x) * GELU_C1 + GELU_C0)) + hx


def _rms_mod(x, norm_w, shift, scale):
    y = x * lax.rsqrt(jnp.mean(x * x, axis=-1, keepdims=True) + RMS_EPS) * norm_w
    return y * (1.0 + scale) + shift


def _lower_bound_row(hlb, layer):
    depth = hlb.shape[0]
    rows = [hlb[j:j + 1] for j in range(depth)]
    mx = functools.reduce(jnp.maximum, rows)
    es = [jnp.exp(r - mx) for r in rows]
    tot = functools.reduce(lambda a, b: a + b, es)
    acc = jnp.zeros_like(rows[0])
    for j in range(1, layer + 1):
        acc = acc + es[j] / tot
    return acc


def _hgrn_tile(q, z, v, lb, st_ref, reverse, need_out=True):
    tl, d = z.shape
    n_chunks = tl // HGRN_CHUNK
    assert n_chunks % 2 == 0
    shift = HGRN_CHUNK.bit_length() - 1
    cs = min(tl, CUMSUM_ROWS)
    r = lax.broadcasted_iota(jnp.int32, (cs, cs), 0)
    c = lax.broadcasted_iota(jnp.int32, (cs, cs), 1)
    same_chunk = jnp.right_shift(r, shift) == jnp.right_shift(c, shift)
    ordered = (c >= r) if reverse else (c <= r)
    tri = jnp.where(jnp.logical_and(same_chunk, ordered), 1.0, 0.0).astype(BF16)
    pair = 2 * HGRN_CHUNK
    pair_mask = jnp.logical_and(same_chunk, ordered)[:pair, :pair]

    e = jnp.exp(-z)
    sig = 1.0 / (1.0 + e)
    f = lb + (1.0 - lb) * sig
    k = 1.0 - f
    g = jnp.log(f)
    g_hi = g.astype(BF16)
    g_lo = (g - g_hi.astype(F32)).astype(BF16)
    b = jnp.concatenate([_dot(tri, g_hi[i:i + cs]) + _dot(tri, g_lo[i:i + cs]) for i in range(0, tl, cs)], axis=0)
    half = HGRN_CHUNK // 2
    ref_i, last_i = (half, 0) if reverse else (half - 1, HGRN_CHUNK - 1)
    refs = [b[n * HGRN_CHUNK + ref_i:n * HGRN_CHUNK + ref_i + 1] for n in range(n_chunks)]
    lasts = [b[n * HGRN_CHUNK + last_i:n * HGRN_CHUNK + last_i + 1] for n in range(n_chunks)]
    ref_rows = jnp.concatenate([jnp.broadcast_to(t, (HGRN_CHUNK, d)) for t in refs], axis=0)
    qs = (q * jnp.exp(b - ref_rows)).astype(BF16) if need_out else None
    ksf = k * jnp.exp(ref_rows - b)
    ks = ksf.astype(BF16)
    vb = v.astype(BF16)

    heads = d // HEAD_DIM
    intra = [None] * n_chunks
    if need_out:
        for n in range(0, n_chunks, 2):
            rs2 = slice(n * HGRN_CHUNK, (n + 2) * HGRN_CHUNK)
            cols2 = []
            for h in range(heads):
                sl = slice(h * HEAD_DIM, (h + 1) * HEAD_DIM)
                sc = _dot(qs[rs2, sl], ksf[rs2, sl].T.astype(BF16))
                cols2.append(_dot(jnp.where(pair_mask, sc, 0.0).astype(BF16), vb[rs2, sl]))
            both = jnp.concatenate(cols2, axis=-1)
            intra[n], intra[n + 1] = both[:HGRN_CHUNK], both[HGRN_CHUNK:]
    states = [st_ref[h] for h in range(heads)]
    outs = [None] * n_chunks
    for n in (reversed(range(n_chunks)) if reverse else range(n_chunks)):
        rs = slice(n * HGRN_CHUNK, (n + 1) * HGRN_CHUNK)
        e_ref = jnp.exp(refs[n]) if need_out else None
        e_last_ref = jnp.exp(lasts[n] - refs[n])
        decay = jnp.exp(lasts[n])
        cols = []
        for h in range(heads):
            sl = slice(h * HEAD_DIM, (h + 1) * HEAD_DIM)
            if need_out:
                cols.append(_dot(qs[rs, sl], (states[h] * e_ref[:, sl]).T.astype(BF16)))
            kv_t = _dot(v[rs, sl].T.astype(BF16), ks[rs, sl])
            states[h] = states[h] * decay[:, sl] + kv_t * e_last_ref[:, sl]
        if need_out:
            outs[n] = jnp.concatenate(cols, axis=-1) + intra[n]
    for h in range(heads):
        st_ref[h] = states[h]
    return jnp.concatenate(outs, axis=0) if need_out else None


def _mod_kernel(c_ref, w_ref, b_ref, o_ref):
    s = _silu(c_ref[...])
    w = w_ref[0]
    s_hi = s.astype(BF16)
    s_lo = (s - s_hi.astype(F32)).astype(BF16)
    w_hi = w.astype(BF16)
    w_lo = (w - w_hi.astype(F32)).astype(BF16)
    o_ref[0] = _dot(s_hi, w_hi) + _dot(s_lo, w_hi) + _dot(s_hi, w_lo) + b_ref[0]


def _modulation(cvec, ada_w, ada_b):
    depth, d, n = ada_w.shape
    rows = cvec.shape[0]
    return pl.pallas_call(
        _mod_kernel,
        grid=(depth, n // MOD_COLS),
        in_specs=[pl.BlockSpec((rows, d), lambda l, j: (0, 0)),
                  pl.BlockSpec((1, d, MOD_COLS), lambda l, j: (l, 0, j)),
                  pl.BlockSpec((1, 1, MOD_COLS), lambda l, j: (l, 0, j))],
        out_specs=pl.BlockSpec((1, rows, MOD_COLS), lambda l, j: (l, 0, j)),
        out_shape=jax.ShapeDtypeStruct((depth, rows, n), F32),
        compiler_params=pltpu.CompilerParams(dimension_semantics=("arbitrary", "arbitrary"),
                                             vmem_limit_bytes=VMEM_LIMIT_BYTES),
        name="adaln_modulation",
    )(cvec, ada_w, ada_b.reshape(depth, 1, n))


def _rev_kernel(x_ref, m_ref, nw_ref, wq_ref, wfb_ref, wi_ref, hlb_ref, s0_ref,
                ob_ref, q_ref, i_ref, st_out_ref,
                st_ref, *, layer):
    d = x_ref.shape[-1]
    j = pl.program_id(1)

    @pl.when(j == 0)
    def _():
        st_ref[...] = s0_ref[0]

    m = m_ref[0]
    h = _rms_mod(x_ref[0], nw_ref[...], m[:, 0:d], m[:, d:2 * d]).astype(BF16)
    q = _silu(_dot(h, wq_ref[...]))
    q_ref[0] = q.astype(BF16)
    zb = _dot(h, wfb_ref[...])
    iv = _dot(h, wi_ref[...])
    i_ref[0] = iv.astype(BF16)
    lb = _lower_bound_row(hlb_ref[...], layer)
    ob_ref[0] = _hgrn_tile(q, zb, iv, lb, st_ref, reverse=True)

    @pl.when(j == pl.num_programs(1) - 1)
    def _():
        st_out_ref[0] = st_ref[...]


def _reverse_sweep(x, mod, norm_w, w_in, hlb, s0, layer):
    bsz, length, d = x.shape
    tl = min(TOKEN_TILE, length)
    nt = length // tl
    heads = d // HEAD_DIM
    mod_rows = mod.shape[0]
    rev = lambda b, j: (b, nt - 1 - j, 0)
    const2 = lambda b, j: (0, 0)
    w_block = lambda n: pl.BlockSpec((d, d), lambda b, j: (0, n))
    state_spec = pl.BlockSpec((1, heads, HEAD_DIM, HEAD_DIM), lambda b, j: (b, 0, 0, 0))
    return pl.pallas_call(
        functools.partial(_rev_kernel, layer=layer),
        grid=(bsz, nt),
        in_specs=[pl.BlockSpec((1, tl, d), rev),
                  pl.BlockSpec((1, 1, N_MOD * d), lambda b, j: (b % mod_rows, 0, 0)),
                  pl.BlockSpec((1, d), const2),
                  w_block(IN_Q), w_block(IN_F_BWD), w_block(IN_I),
                  pl.BlockSpec(hlb.shape, const2),
                  state_spec],
        out_specs=[pl.BlockSpec((1, tl, d), rev),
                   pl.BlockSpec((1, tl, d), rev),
                   pl.BlockSpec((1, tl, d), rev),
                   state_spec],
        out_shape=[jax.ShapeDtypeStruct((bsz, length, d), F32),
                   jax.ShapeDtypeStruct((bsz, length, d), BF16),
                   jax.ShapeDtypeStruct((bsz, length, d), BF16),
                   jax.ShapeDtypeStruct((bsz, heads, HEAD_DIM, HEAD_DIM), F32)],
        scratch_shapes=[pltpu.VMEM((heads, HEAD_DIM, HEAD_DIM), F32)],
        compiler_params=pltpu.CompilerParams(dimension_semantics=("arbitrary", "arbitrary"),
                                             vmem_limit_bytes=VMEM_LIMIT_BYTES),
        name="hgrn_reverse_sweep",
    )(x, mod, norm_w, w_in, w_in, w_in, hlb, s0)


def _state_kernel(x_ref, m_ref, nw_ref, wf_ref, wi_ref, hlb_ref, s0_ref, st_out_ref, st_ref, *, layer, reverse):
    d = x_ref.shape[-1]
    j = pl.program_id(1)

    @pl.when(j == 0)
    def _():
        st_ref[...] = s0_ref[0]

    m = m_ref[0]
    h = _rms_mod(x_ref[0], nw_ref[...], m[:, 0:d], m[:, d:2 * d]).astype(BF16)
    lb = _lower_bound_row(hlb_ref[...], layer)
    _hgrn_tile(None, _dot(h, wf_ref[...]), _dot(h, wi_ref[...]), lb, st_ref, reverse, need_out=False)

    @pl.when(j == pl.num_programs(1) - 1)
    def _():
        st_out_ref[0] = st_ref[...]


def _state_sweep(x, mod, norm_w, w_in, hlb, s0, layer, reverse):
    bsz, length, d = x.shape
    tl = min(TOKEN_TILE, length)
    nt = length // tl
    heads = d // HEAD_DIM
    mod_rows = mod.shape[0]
    tile = (lambda b, j: (b, nt - 1 - j, 0)) if reverse else (lambda b, j: (b, j, 0))
    const2 = lambda b, j: (0, 0)
    w_block = lambda n: pl.BlockSpec((d, d), lambda b, j: (0, n))
    state_spec = pl.BlockSpec((1, heads, HEAD_DIM, HEAD_DIM), lambda b, j: (b, 0, 0, 0))
    return pl.pallas_call(
        functools.partial(_state_kernel, layer=layer, reverse=reverse),
        grid=(bsz, nt),
        in_specs=[pl.BlockSpec((1, tl, d), tile),
                  pl.BlockSpec((1, 1, N_MOD * d), lambda b, j: (b % mod_rows, 0, 0)),
                  pl.BlockSpec((1, d), const2),
                  w_block(IN_F_BWD if reverse else IN_F_FWD), w_block(IN_I),
                  pl.BlockSpec(hlb.shape, const2),
                  state_spec],
        out_specs=state_spec,
        out_shape=jax.ShapeDtypeStruct((bsz, heads, HEAD_DIM, HEAD_DIM), F32),
        scratch_shapes=[pltpu.VMEM((heads, HEAD_DIM, HEAD_DIM), F32)],
        compiler_params=pltpu.CompilerParams(dimension_semantics=("arbitrary", "arbitrary"),
                                             vmem_limit_bytes=VMEM_LIMIT_BYTES),
        name="hgrn_state_sweep",
    )(x, mod, norm_w, w_in, w_in, hlb, s0)


def _fwd_kernel(x_ref, m_ref, nw_ref, wff_ref, wu_ref, wv_ref, wog_ref, wga_ref, wgb_ref,
                hlb_ref, s0_ref, q_ref, i_ref, ob_ref,
                lnw_ref, lnb_ref, sw_ref, sb_ref, hnw_ref, wa_ref, wb_ref, wo_ref,
                xo_ref, st_out_ref,
                st_ref, *, layer):
    d = x_ref.shape[-1]
    tl = x_ref.shape[1]
    j = pl.program_id(1)

    @pl.when(j == 0)
    def _():
        st_ref[...] = s0_ref[0]

    x = x_ref[0]
    m = m_ref[0]
    h = _rms_mod(x, nw_ref[...], m[:, 0:d], m[:, d:2 * d]).astype(BF16)

    lb = _lower_bound_row(hlb_ref[...], layer)
    z_f = _dot(h, wff_ref[...])
    v_proj = _dot(h, wv_ref[...])
    o_f = _hgrn_tile(q_ref[0].astype(F32), z_f, i_ref[0].astype(F32), lb, st_ref, reverse=False)

    og = _dot(h, wog_ref[...])
    u = _dot(h, wu_ref[...])
    gate_a = _dot(h, wga_ref[...])
    gate_b = _dot(h, wgb_ref[...])

    o = o_f + ob_ref[0]
    parts = []
    for hh in range(d // HEAD_DIM):
        oh = o[:, hh * HEAD_DIM:(hh + 1) * HEAD_DIM]
        parts.append(oh * lax.rsqrt(jnp.mean(oh * oh, axis=-1, keepdims=True) + RMS_EPS))
    y_b = (jnp.concatenate(parts, axis=-1) * hnw_ref[...]) * _silu(og)
    t_b = _dot(y_b.astype(BF16), wb_ref[...])

    gv = _gelu_tanh(v_proj)
    mu = jnp.mean(gv, axis=-1, keepdims=True)
    cen = gv - mu
    var = jnp.mean(cen * cen, axis=-1, keepdims=True)
    vn = (cen * lax.rsqrt(var + LN_EPS) * lnw_ref[...] + lnb_ref[...]).astype(BF16)
    gd = d // SGU_GROUPS
    row_blocks = []
    for n in range(0, tl // SGU_CHUNK, 2):
        r0 = slice(n * SGU_CHUNK, (n + 1) * SGU_CHUNK)
        r1 = slice((n + 1) * SGU_CHUNK, (n + 2) * SGU_CHUNK)
        two = [_dot(sw_ref[g], jnp.concatenate([vn[r0, g * gd:(g + 1) * gd], vn[r1, g * gd:(g + 1) * gd]], axis=-1))
               for g in range(SGU_GROUPS)]
        row_blocks.append(jnp.concatenate([t[:, :gd] for t in two], axis=-1) + sb_ref[...])
        row_blocks.append(jnp.concatenate([t[:, gd:] for t in two], axis=-1) + sb_ref[...])
    mixed = jnp.concatenate(row_blocks, axis=0)
    y_a = _gelu_tanh_gate(u, mixed)
    t_a = _dot(y_a.astype(BF16), wa_ref[...])

    merged = _sigmoid(gate_a) * t_a + _sigmoid(gate_b) * t_b
    y = _dot(merged.astype(BF16), wo_ref[...])
    xo_ref[0] = x + m[:, 2 * d:3 * d] * y

    @pl.when(j == pl.num_programs(1) - 1)
    def _():
        st_out_ref[0] = st_ref[...]


def _forward_sweep(x, mod, norm_w, w_in, hlb, s0, q, iv, ob, ln_w, ln_b, sgu_w, sgu_bias,
                   hnorm_w, w_a, w_b, w_o, layer):
    bsz, length, d = x.shape
    tl = min(TOKEN_TILE, length)
    nt = length // tl
    heads = d // HEAD_DIM
    mod_rows = mod.shape[0]
    tok = lambda b, j: (b, j, 0)
    const2 = lambda b, j: (0, 0)
    const3 = lambda b, j: (0, 0, 0)
    once = dict(pipeline_mode=pl.Buffered(1))
    w_block = lambda n: pl.BlockSpec((d, d), lambda b, j: (0, n), **once)
    state_spec = pl.BlockSpec((1, heads, HEAD_DIM, HEAD_DIM), lambda b, j: (b, 0, 0, 0))
    return pl.pallas_call(
        functools.partial(_fwd_kernel, layer=layer),
        grid=(bsz, nt),
        in_specs=[pl.BlockSpec((1, tl, d), tok),
                  pl.BlockSpec((1, 1, N_MOD * d), lambda b, j: (b % mod_rows, 0, 0)),
                  pl.BlockSpec((1, d), const2),
                  w_block(IN_F_FWD), w_block(IN_U), w_block(IN_V), w_block(IN_OG), w_block(IN_GATE_A),
                  w_block(IN_GATE_B),
                  pl.BlockSpec(hlb.shape, const2),
                  state_spec,
                  pl.BlockSpec((1, tl, d), tok),
                  pl.BlockSpec((1, tl, d), tok),
                  pl.BlockSpec((1, tl, d), tok),
                  pl.BlockSpec((1, d), const2),
                  pl.BlockSpec((1, d), const2),
                  pl.BlockSpec(sgu_w.shape, const3, **once),
                  pl.BlockSpec(sgu_bias.shape, const2, **once),
                  pl.BlockSpec((1, d), const2),
                  pl.BlockSpec((d, d), const2, **once),
                  pl.BlockSpec((d, d), const2, **once),
                  pl.BlockSpec((d, d), const2, **once)],
        out_specs=[pl.BlockSpec((1, tl, d), tok), state_spec],
        out_shape=[jax.ShapeDtypeStruct((bsz, length, d), F32),
                   jax.ShapeDtypeStruct((bsz, heads, HEAD_DIM, HEAD_DIM), F32)],
        scratch_shapes=[pltpu.VMEM((heads, HEAD_DIM, HEAD_DIM), F32)],
        compiler_params=pltpu.CompilerParams(dimension_semantics=("arbitrary", "arbitrary"),
                                             vmem_limit_bytes=VMEM_LIMIT_BYTES),
        name="token_mixer_forward_sweep",
    )(x, mod, norm_w, w_in, w_in, w_in, w_in, w_in, w_in, hlb, s0, q, iv, ob, ln_w, ln_b, sgu_w, sgu_bias,
      hnorm_w, w_a, w_b, w_o)


def _ffn_kernel(xp_ref, x_ref, xn_ref, m_ref, nw_ref, wu_ref, cw_ref, cb_ref, wd_ref, fw_ref,
                xo_ref, h_ref, *, on_grid, final_norm):
    d = x_ref.shape[-1]
    tl = x_ref.shape[1]
    halo = xp_ref.shape[1]
    d_ff = wd_ref.shape[0]
    j = pl.program_id(1)
    m = m_ref[0]
    shift, scale, gate = m[:, 3 * d:4 * d], m[:, 4 * d:5 * d], m[:, 5 * d:6 * d]
    x = x_ref[0]
    h_ref[halo:halo + tl, :] = _rms_mod(x, nw_ref[...], shift, scale).astype(BF16)
    if on_grid:
        h_prev = _rms_mod(xp_ref[0], nw_ref[...], shift, scale)
        h_next = _rms_mod(xn_ref[0], nw_ref[...], shift, scale)
        h_ref[0:halo, :] = jnp.where(j > 0, h_prev, 0.0).astype(BF16)
        h_ref[halo + tl:, :] = jnp.where(j < pl.num_programs(1) - 1, h_next, 0.0).astype(BF16)
        period = GRID_W
    else:
        period = tl
    assert period & (period - 1) == 0
    col = jnp.bitwise_and(lax.broadcasted_iota(jnp.int32, (tl, 1), 0), period - 1)
    has_left = col != 0
    has_right = col != period - 1

    h_main = h_ref[halo:halo + tl, :]
    cw = cw_ref[...]
    if on_grid:
        a = _dot(h_ref[...], wu_ref[:, 0:d_ff].astype(BF16))
        mid = a[halo:halo + tl]
        up = a[halo - GRID_W:halo - GRID_W + tl]
        dn = a[halo + GRID_W:halo + GRID_W + tl]
    else:
        mid = _dot(h_main, wu_ref[:, 0:d_ff].astype(BF16))
    v = _dot(h_main, wu_ref[:, d_ff:2 * d_ff].astype(BF16))
    if on_grid:
        p = [cw[dx:dx + 1] * up + cw[3 + dx:4 + dx] * mid + cw[6 + dx:7 + dx] * dn for dx in range(CONV_W)]
    else:
        p = [cw[3 + dx:4 + dx] * mid for dx in range(CONV_W)]
    left = jnp.where(has_left, pltpu.roll(p[0], 1, axis=0), 0.0)
    right = jnp.where(has_right, pltpu.roll(p[2], tl - 1, axis=0), 0.0)
    conv = left + p[1] + right + cb_ref[...]
    act = _gelu_tanh_gate(conv, v).astype(BF16)
    out = x + gate * _dot(act, wd_ref[...].astype(BF16))
    if final_norm:
        out = out * lax.rsqrt(jnp.mean(out * out, axis=-1, keepdims=True) + RMS_EPS) * fw_ref[...]
    xo_ref[0] = out


def _conv_ffn(x, mod, norm_w, w_up, conv_w, conv_b, w_down, final_w, layer, on_grid, final_norm):
    bsz, length, d = x.shape
    tl = min(FFN_TILE, length)
    nt = length // tl
    assert on_grid or nt == 1, "the sequence form of the conv keeps the whole sequence in one tile"
    halo = GRID_W
    d_ff = w_down.shape[1]
    mod_rows = mod.shape[0]
    hb = tl // halo
    n_hb = length // halo
    const2 = lambda b, j: (0, 0)
    once = dict(pipeline_mode=pl.Buffered(1))
    return pl.pallas_call(
        functools.partial(_ffn_kernel, on_grid=on_grid, final_norm=final_norm),
        grid=(bsz, nt),
        in_specs=[pl.BlockSpec((1, halo, d), lambda b, j: (b, jnp.maximum(j * hb - 1, 0), 0)),
                  pl.BlockSpec((1, tl, d), lambda b, j: (b, j, 0)),
                  pl.BlockSpec((1, halo, d), lambda b, j: (b, jnp.minimum((j + 1) * hb, n_hb - 1), 0)),
                  pl.BlockSpec((1, 1, N_MOD * d), lambda b, j: (b % mod_rows, 0, 0)),
                  pl.BlockSpec((1, d), const2),
                  pl.BlockSpec((None, d, 2 * d_ff), lambda b, j: (layer, 0, 0), **once),
                  pl.BlockSpec((CONV_W * CONV_W, d_ff), const2, **once),
                  pl.BlockSpec((1, d_ff), const2, **once),
                  pl.BlockSpec((None, d_ff, d), lambda b, j: (layer, 0, 0), **once),
                  pl.BlockSpec((1, d), const2)],
        out_specs=pl.BlockSpec((1, tl, d), lambda b, j: (b, j, 0)),
        out_shape=jax.ShapeDtypeStruct((bsz, length, d), F32),
        scratch_shapes=[pltpu.VMEM((tl + 2 * halo, d), BF16)],
        compiler_params=pltpu.CompilerParams(dimension_semantics=("arbitrary", "arbitrary"),
                                             vmem_limit_bytes=VMEM_LIMIT_BYTES),
        name="conv_ffn",
    )(x, x, x, mod, norm_w, w_up, conv_w, conv_b, w_down, final_w)


def kernel(x, c, ctx, c_ctx, ada_w, ada_b, norm1_w, w_in, sgu_ln_w, sgu_ln_b, sgu_w, sgu_b,
           hgrn_lower_bounds, hgrn_norm_w, w_branch_a, w_branch_b, w_out, norm2_w,
           ffn_w_up, ffn_conv_w, ffn_conv_b, ffn_w_down, final_norm_w):
    depth = ada_w.shape[0]
    bsz, _, d = x.shape
    heads = d // HEAD_DIM
    d_ff = ffn_w_down.shape[1]
    assert (N_MOD * d) % MOD_COLS == 0

    pad = (-(bsz + 1)) % 8
    cvec = jnp.concatenate([c, c_ctx[None, :], jnp.zeros((pad, d), F32)], axis=0)
    mod = _modulation(cvec, ada_w, ada_b)
    mod_x = mod[:, :bsz, None, :]
    mod_c = mod[:, bsz:bsz + 1, None, :]

    hlb_f = hgrn_lower_bounds[:, :d]
    hlb_b = hgrn_lower_bounds[:, d:]
    zero_state = jnp.zeros((bsz, heads, HEAD_DIM, HEAD_DIM), F32)
    row = lambda t: t[None, :]

    for l in range(depth):
        last = l == depth - 1
        wl = w_in[l].astype(BF16)
        sw = sgu_w[l].astype(BF16)
        sgu_bias = jnp.repeat(sgu_b[l].T, d // SGU_GROUPS, axis=1)
        hnw = jnp.tile(hgrn_norm_w[l], heads)[None, :]
        w_a, w_b, w_o = (t[l].astype(BF16) for t in (w_branch_a, w_branch_b, w_out))
        conv_w = ffn_conv_w[l].reshape(CONV_W * CONV_W, d_ff)
        conv_b = ffn_conv_b[l][None, :]

        mixer = functools.partial(_forward_sweep, ln_w=row(sgu_ln_w[l]), ln_b=row(sgu_ln_b[l]), sgu_w=sw,
                                  sgu_bias=sgu_bias, hnorm_w=hnw, w_a=w_a, w_b=w_b, w_o=w_o, layer=l)
        ffn = functools.partial(_conv_ffn, norm_w=row(norm2_w[l]), w_up=ffn_w_up, conv_w=conv_w,
                                conv_b=conv_b, w_down=ffn_w_down, final_w=row(final_norm_w), layer=l)

        if last:
            s_b = _state_sweep(ctx, mod_c[l], row(norm1_w[l]), wl, hlb_b, zero_state, l, reverse=True)
            s_f = _state_sweep(ctx, mod_c[l], row(norm1_w[l]), wl, hlb_f, zero_state, l, reverse=False)
        else:
            ob_c, q_c, i_c, s_b = _reverse_sweep(ctx, mod_c[l], row(norm1_w[l]), wl, hlb_b, zero_state, l)
            ctx_mixed, s_f = mixer(ctx, mod_c[l], row(norm1_w[l]), wl, hlb_f, zero_state, q_c, i_c, ob_c)
        ob_x, q_x, i_x, _ = _reverse_sweep(x, mod_x[l], row(norm1_w[l]), wl, hlb_b, s_b, l)
        x, _ = mixer(x, mod_x[l], row(norm1_w[l]), wl, hlb_f, s_f, q_x, i_x, ob_x)
        x = ffn(x, mod_x[l], on_grid=True, final_norm=last)
        if not last:
            ctx = ffn(ctx_mixed, mod_c[l], on_grid=False, final_norm=False)
    return x
```

```python
import functools

import jax
import jax.numpy as jnp
from jax import lax
from jax.experimental import pallas as pl
from jax.experimental.pallas import tpu as pltpu

F32 = jnp.float32
BF16 = jnp.bfloat16

N_MOD = 6
RMS_EPS = 1e-6
LN_EPS = 1e-5
HEAD_DIM = 128
HGRN_CHUNK = 64
CUMSUM_ROWS = 256
SGU_CHUNK = 128
SGU_GROUPS = 8
GRID_W = 64
CONV_W = 3
IN_Q, IN_F_FWD, IN_F_BWD, IN_I, IN_U, IN_V, IN_OG, IN_GATE_A, IN_GATE_B = range(9)

TOKEN_TILE = 512
FFN_TILE = 512
MOD_COLS = 1536
VMEM_LIMIT_BYTES = 56 * 1024 * 1024


def _dot(a, b):
    return jnp.dot(a, b, preferred_element_type=F32)


def _sigmoid(x):
    return 1.0 / (1.0 + jnp.exp(-x))


def _silu(x):
    return x * _sigmoid(x)


GELU_C0 = 0.7978845608028654
GELU_C1 = GELU_C0 * 0.044715


def _gelu_tanh_gate(x, gate):
    hg = (0.5 * x) * gate
    return hg * jnp.tanh(x * ((x * x) * GELU_C1 + GELU_C0)) + hg


def _gelu_tanh(x):
    hx = 0.5 * x
    return hx * jnp.tanh(x * ((x * x) * GELU_C1 + GELU_C0)) + hx


def _rms_mod(x, norm_w, shift, scale):
    y = x * lax.rsqrt(jnp.mean(x * x, axis=-1, keepdims=True) + RMS_EPS) * norm_w
    return y * (1.0 + scale) + shift


def _lower_bound_row(hlb, layer):
    depth = hlb.shape[0]
    rows = [hlb[j:j + 1] for j in range(depth)]
    mx = functools.reduce(jnp.maximum, rows)
    es = [jnp.exp(r - mx) for r in rows]
    tot = functools.reduce(lambda a, b: a + b, es)
    acc = jnp.zeros_like(rows[0])
    for j in range(1, layer + 1):
        acc = acc + es[j] / tot
    return acc


def _hgrn_tile(q, z, v, lb, st_ref, reverse, need_out=True):
    tl, d = z.shape
    n_chunks = tl // HGRN_CHUNK
    assert n_chunks % 2 == 0
    shift = HGRN_CHUNK.bit_length() - 1
    cs = min(tl, CUMSUM_ROWS)
    r = lax.broadcasted_iota(jnp.int32, (cs, cs), 0)
    c = lax.broadcasted_iota(jnp.int32, (cs, cs), 1)
    same_chunk = jnp.right_shift(r, shift) == jnp.right_shift(c, shift)
    ordered = (c >= r) if reverse else (c <= r)
    tri = jnp.where(jnp.logical_and(same_chunk, ordered), 1.0, 0.0).astype(BF16)
    pair = 2 * HGRN_CHUNK
    pair_mask = jnp.logical_and(same_chunk, ordered)[:pair, :pair]

    e = jnp.exp(-z)
    sig = 1.0 / (1.0 + e)
    f = lb + (1.0 - lb) * sig
    k = 1.0 - f
    g = jnp.log(f)
    g_hi = g.astype(BF16)
    g_lo = (g - g_hi.astype(F32)).astype(BF16)
    b = jnp.concatenate([_dot(tri, g_hi[i:i + cs]) + _dot(tri, g_lo[i:i + cs]) for i in range(0, tl, cs)], axis=0)
    half = HGRN_CHUNK // 2
    ref_i, last_i = (half, 0) if reverse else (half - 1, HGRN_CHUNK - 1)
    refs = [b[n * HGRN_CHUNK + ref_i:n * HGRN_CHUNK + ref_i + 1] for n in range(n_chunks)]
    lasts = [b[n * HGRN_CHUNK + last_i:n * HGRN_CHUNK + last_i + 1] for n in range(n_chunks)]
    ref_rows = jnp.concatenate([jnp.broadcast_to(t, (HGRN_CHUNK, d)) for t in refs], axis=0)
    qs = (q * jnp.exp(b - ref_rows)).astype(BF16) if need_out else None
    ksf = k * jnp.exp(ref_rows - b)
    ks = ksf.astype(BF16)
    vb = v.astype(BF16)

    heads = d // HEAD_DIM
    intra = [None] * n_chunks
    if need_out:
        for n in range(0, n_chunks, 2):
            rs2 = slice(n * HGRN_CHUNK, (n + 2) * HGRN_CHUNK)
            cols2 = []
            for h in range(heads):
                sl = slice(h * HEAD_DIM, (h + 1) * HEAD_DIM)
                sc = _dot(qs[rs2, sl], ksf[rs2, sl].T.astype(BF16))
                cols2.append(_dot(jnp.where(pair_mask, sc, 0.0).astype(BF16), vb[rs2, sl]))
            both = jnp.concatenate(cols2, axis=-1)
            intra[n], intra[n + 1] = both[:HGRN_CHUNK], both[HGRN_CHUNK:]
    states = [st_ref[h] for h in range(heads)]
    outs = [None] * n_chunks
    for n in (reversed(range(n_chunks)) if reverse else range(n_chunks)):
        rs = slice(n * HGRN_CHUNK, (n + 1) * HGRN_CHUNK)
        e_ref = jnp.exp(refs[n]) if need_out else None
        e_last_ref = jnp.exp(lasts[n] - refs[n])
        decay = jnp.exp(lasts[n])
        cols = []
        lanes = [slice(h * HEAD_DIM, (h + 1) * HEAD_DIM) for h in range(heads)]
        kv_t = [_dot(v[rs, sl].T.astype(BF16), ks[rs, sl]) for sl in lanes]
        for h in range(heads):
            sl = lanes[h]
            if need_out:
                cols.append(_dot(qs[rs, sl], (states[h] * e_ref[:, sl]).T.astype(BF16)))
            states[h] = states[h] * decay[:, sl] + kv_t[h] * e_last_ref[:, sl]
        if need_out:
            outs[n] = jnp.concatenate(cols, axis=-1) + intra[n]
    for h in range(heads):
        st_ref[h] = states[h]
    return jnp.concatenate(outs, axis=0) if need_out else None


def _mod_kernel(c_ref, w_ref, b_ref, o_ref):
    s = _silu(c_ref[...])
    w = w_ref[0]
    s_hi = s.astype(BF16)
    s_lo = (s - s_hi.astype(F32)).astype(BF16)
    w_hi = w.astype(BF16)
    w_lo = (w - w_hi.astype(F32)).astype(BF16)
    o_ref[0] = _dot(s_hi, w_hi) + _dot(s_lo, w_hi) + _dot(s_hi, w_lo) + b_ref[0]


def _modulation(cvec, ada_w, ada_b):
    depth, d, n = ada_w.shape
    rows = cvec.shape[0]
    return pl.pallas_call(
        _mod_kernel,
        grid=(depth, n // MOD_COLS),
        in_specs=[pl.BlockSpec((rows, d), lambda l, j: (0, 0)),
                  pl.BlockSpec((1, d, MOD_COLS), lambda l, j: (l, 0, j)),
                  pl.BlockSpec((1, 1, MOD_COLS), lambda l, j: (l, 0, j))],
        out_specs=pl.BlockSpec((1, rows, MOD_COLS), lambda l, j: (l, 0, j)),
        out_shape=jax.ShapeDtypeStruct((depth, rows, n), F32),
        compiler_params=pltpu.CompilerParams(dimension_semantics=("arbitrary", "arbitrary"),
                                             vmem_limit_bytes=VMEM_LIMIT_BYTES),
        name="adaln_modulation",
    )(cvec, ada_w, ada_b.reshape(depth, 1, n))


def _rev_kernel(x_ref, m_ref, nw_ref, wq_ref, wfb_ref, wi_ref, hlb_ref, s0_ref,
                ob_ref, q_ref, i_ref, st_out_ref,
                st_ref, *, layer):
    d = x_ref.shape[-1]
    j = pl.program_id(1)

    @pl.when(j == 0)
    def _():
        st_ref[...] = s0_ref[0]

    m = m_ref[0]
    h = _rms_mod(x_ref[0], nw_ref[...], m[:, 0:d], m[:, d:2 * d]).astype(BF16)
    q = _silu(_dot(h, wq_ref[...]))
    q_ref[0] = q.astype(BF16)
    zb = _dot(h, wfb_ref[...])
    iv = _dot(h, wi_ref[...])
    i_ref[0] = iv.astype(BF16)
    lb = _lower_bound_row(hlb_ref[...], layer)
    ob_ref[0] = _hgrn_tile(q, zb, iv, lb, st_ref, reverse=True)

    @pl.when(j == pl.num_programs(1) - 1)
    def _():
        st_out_ref[0] = st_ref[...]


def _reverse_sweep(x, mod, norm_w, w_in, hlb, s0, layer):
    bsz, length, d = x.shape
    tl = min(TOKEN_TILE, length)
    nt = length // tl
    heads = d // HEAD_DIM
    mod_rows = mod.shape[0]
    rev = lambda b, j: (b, nt - 1 - j, 0)
    const2 = lambda b, j: (0, 0)
    w_block = lambda n: pl.BlockSpec((d, d), lambda b, j: (0, n))
    state_spec = pl.BlockSpec((1, heads, HEAD_DIM, HEAD_DIM), lambda b, j: (b, 0, 0, 0))
    return pl.pallas_call(
        functools.partial(_rev_kernel, layer=layer),
        grid=(bsz, nt),
        in_specs=[pl.BlockSpec((1, tl, d), rev),
                  pl.BlockSpec((1, 1, N_MOD * d), lambda b, j: (b % mod_rows, 0, 0)),
                  pl.BlockSpec((1, d), const2),
                  w_block(IN_Q), w_block(IN_F_BWD), w_block(IN_I),
                  pl.BlockSpec(hlb.shape, const2),
                  state_spec],
        out_specs=[pl.BlockSpec((1, tl, d), rev),
                   pl.BlockSpec((1, tl, d), rev),
                   pl.BlockSpec((1, tl, d), rev),
                   state_spec],
        out_shape=[jax.ShapeDtypeStruct((bsz, length, d), F32),
                   jax.ShapeDtypeStruct((bsz, length, d), BF16),
                   jax.ShapeDtypeStruct((bsz, length, d), BF16),
                   jax.ShapeDtypeStruct((bsz, heads, HEAD_DIM, HEAD_DIM), F32)],
        scratch_shapes=[pltpu.VMEM((heads, HEAD_DIM, HEAD_DIM), F32)],
        compiler_params=pltpu.CompilerParams(dimension_semantics=("arbitrary", "arbitrary"),
                                             vmem_limit_bytes=VMEM_LIMIT_BYTES),
        name="hgrn_reverse_sweep",
    )(x, mod, norm_w, w_in, w_in, w_in, hlb, s0)


def _state_kernel(x_ref, m_ref, nw_ref, wf_ref, wi_ref, hlb_ref, s0_ref, st_out_ref, st_ref, *, layer, reverse):
    d = x_ref.shape[-1]
    j = pl.program_id(1)

    @pl.when(j == 0)
    def _():
        st_ref[...] = s0_ref[0]

    m = m_ref[0]
    h = _rms_mod(x_ref[0], nw_ref[...], m[:, 0:d], m[:, d:2 * d]).astype(BF16)
    lb = _lower_bound_row(hlb_ref[...], layer)
    _hgrn_tile(None, _dot(h, wf_ref[...]), _dot(h, wi_ref[...]), lb, st_ref, reverse, need_out=False)

    @pl.when(j == pl.num_programs(1) - 1)
    def _():
        st_out_ref[0] = st_ref[...]


def _state_sweep(x, mod, norm_w, w_in, hlb, s0, layer, reverse):
    bsz, length, d = x.shape
    tl = min(TOKEN_TILE, length)
    nt = length // tl
    heads = d // HEAD_DIM
    mod_rows = mod.shape[0]
    tile = (lambda b, j: (b, nt - 1 - j, 0)) if reverse else (lambda b, j: (b, j, 0))
    const2 = lambda b, j: (0, 0)
    w_block = lambda n: pl.BlockSpec((d, d), lambda b, j: (0, n))
    state_spec = pl.BlockSpec((1, heads, HEAD_DIM, HEAD_DIM), lambda b, j: (b, 0, 0, 0))
    return pl.pallas_call(
        functools.partial(_state_kernel, layer=layer, reverse=reverse),
        grid=(bsz, nt),
        in_specs=[pl.BlockSpec((1, tl, d), tile),
                  pl.BlockSpec((1, 1, N_MOD * d), lambda b, j: (b % mod_rows, 0, 0)),
                  pl.BlockSpec((1, d), const2),
                  w_block(IN_F_BWD if reverse else IN_F_FWD), w_block(IN_I),
                  pl.BlockSpec(hlb.shape, const2),
                  state_spec],
        out_specs=state_spec,
        out_shape=jax.ShapeDtypeStruct((bsz, heads, HEAD_DIM, HEAD_DIM), F32),
        scratch_shapes=[pltpu.VMEM((heads, HEAD_DIM, HEAD_DIM), F32)],
        compiler_params=pltpu.CompilerParams(dimension_semantics=("arbitrary", "arbitrary"),
                                             vmem_limit_bytes=VMEM_LIMIT_BYTES),
        name="hgrn_state_sweep",
    )(x, mod, norm_w, w_in, w_in, hlb, s0)


def _fwd_kernel(x_ref, m_ref, nw_ref, wff_ref, wu_ref, wv_ref, wog_ref, wga_ref, wgb_ref,
                hlb_ref, s0_ref, q_ref, i_ref, ob_ref,
                lnw_ref, lnb_ref, sw_ref, sb_ref, hnw_ref, wa_ref, wb_ref, wo_ref,
                xo_ref, st_out_ref,
                st_ref, *, layer):
    d = x_ref.shape[-1]
    tl = x_ref.shape[1]
    j = pl.program_id(1)

    @pl.when(j == 0)
    def _():
        st_ref[...] = s0_ref[0]

    x = x_ref[0]
    m = m_ref[0]
    h = _rms_mod(x, nw_ref[...], m[:, 0:d], m[:, d:2 * d]).astype(BF16)

    lb = _lower_bound_row(hlb_ref[...], layer)
    z_f = _dot(h, wff_ref[...])
    v_proj = _dot(h, wv_ref[...])
    o_f = _hgrn_tile(q_ref[0].astype(F32), z_f, i_ref[0].astype(F32), lb, st_ref, reverse=False)

    og = _dot(h, wog_ref[...])
    u = _dot(h, wu_ref[...])
    gate_a = _dot(h, wga_ref[...])
    gate_b = _dot(h, wgb_ref[...])

    o = o_f + ob_ref[0]
    parts = []
    for hh in range(d // HEAD_DIM):
        oh = o[:, hh * HEAD_DIM:(hh + 1) * HEAD_DIM]
        parts.append(oh * lax.rsqrt(jnp.mean(oh * oh, axis=-1, keepdims=True) + RMS_EPS))
    y_b = (jnp.concatenate(parts, axis=-1) * hnw_ref[...]) * _silu(og)
    t_b = _dot(y_b.astype(BF16), wb_ref[...])

    gv = _gelu_tanh(v_proj)
    mu = jnp.mean(gv, axis=-1, keepdims=True)
    cen = gv - mu
    var = jnp.mean(cen * cen, axis=-1, keepdims=True)
    vn = (cen * lax.rsqrt(var + LN_EPS) * lnw_ref[...] + lnb_ref[...]).astype(BF16)
    gd = d // SGU_GROUPS
    row_blocks = []
    for n in range(0, tl // SGU_CHUNK, 2):
        r0 = slice(n * SGU_CHUNK, (n + 1) * SGU_CHUNK)
        r1 = slice((n + 1) * SGU_CHUNK, (n + 2) * SGU_CHUNK)
        two = [_dot(sw_ref[g], jnp.concatenate([vn[r0, g * gd:(g + 1) * gd], vn[r1, g * gd:(g + 1) * gd]], axis=-1))
               for g in range(SGU_GROUPS)]
        row_blocks.append(jnp.concatenate([t[:, :gd] for t in two], axis=-1) + sb_ref[...])
        row_blocks.append(jnp.concatenate([t[:, gd:] for t in two], axis=-1) + sb_ref[...])
    mixed = jnp.concatenate(row_blocks, axis=0)
    y_a = _gelu_tanh_gate(u, mixed)
    t_a = _dot(y_a.astype(BF16), wa_ref[...])

    merged = _sigmoid(gate_a) * t_a + _sigmoid(gate_b) * t_b
    y = _dot(merged.astype(BF16), wo_ref[...])
    xo_ref[0] = x + m[:, 2 * d:3 * d] * y

    @pl.when(j == pl.num_programs(1) - 1)
    def _():
        st_out_ref[0] = st_ref[...]


def _forward_sweep(x, mod, norm_w, w_in, hlb, s0, q, iv, ob, ln_w, ln_b, sgu_w, sgu_bias,
                   hnorm_w, w_a, w_b, w_o, layer):
    bsz, length, d = x.shape
    tl = min(TOKEN_TILE, length)
    nt = length // tl
    heads = d // HEAD_DIM
    mod_rows = mod.shape[0]
    tok = lambda b, j: (b, j, 0)
    const2 = lambda b, j: (0, 0)
    const3 = lambda b, j: (0, 0, 0)
    once = dict(pipeline_mode=pl.Buffered(1))
    w_block = lambda n: pl.BlockSpec((d, d), lambda b, j: (0, n), **once)
    state_spec = pl.BlockSpec((1, heads, HEAD_DIM, HEAD_DIM), lambda b, j: (b, 0, 0, 0))
    return pl.pallas_call(
        functools.partial(_fwd_kernel, layer=layer),
        grid=(bsz, nt),
        in_specs=[pl.BlockSpec((1, tl, d), tok),
                  pl.BlockSpec((1, 1, N_MOD * d), lambda b, j: (b % mod_rows, 0, 0)),
                  pl.BlockSpec((1, d), const2),
                  w_block(IN_F_FWD), w_block(IN_U), w_block(IN_V), w_block(IN_OG), w_block(IN_GATE_A),
                  w_block(IN_GATE_B),
                  pl.BlockSpec(hlb.shape, const2),
                  state_spec,
                  pl.BlockSpec((1, tl, d), tok),
                  pl.BlockSpec((1, tl, d), tok),
                  pl.BlockSpec((1, tl, d), tok),
                  pl.BlockSpec((1, d), const2),
                  pl.BlockSpec((1, d), const2),
                  pl.BlockSpec(sgu_w.shape, const3, **once),
                  pl.BlockSpec(sgu_bias.shape, const2, **once),
                  pl.BlockSpec((1, d), const2),
                  pl.BlockSpec((d, d), const2, **once),
                  pl.BlockSpec((d, d), const2, **once),
                  pl.BlockSpec((d, d), const2, **once)],
        out_specs=[pl.BlockSpec((1, tl, d), tok), state_spec],
        out_shape=[jax.ShapeDtypeStruct((bsz, length, d), F32),
                   jax.ShapeDtypeStruct((bsz, heads, HEAD_DIM, HEAD_DIM), F32)],
        scratch_shapes=[pltpu.VMEM((heads, HEAD_DIM, HEAD_DIM), F32)],
        compiler_params=pltpu.CompilerParams(dimension_semantics=("arbitrary", "arbitrary"),
                                             vmem_limit_bytes=VMEM_LIMIT_BYTES),
        name="token_mixer_forward_sweep",
    )(x, mod, norm_w, w_in, w_in, w_in, w_in, w_in, w_in, hlb, s0, q, iv, ob, ln_w, ln_b, sgu_w, sgu_bias,
      hnorm_w, w_a, w_b, w_o)


def _ffn_kernel(xp_ref, x_ref, xn_ref, m_ref, nw_ref, wu_ref, cw_ref, cb_ref, wd_ref, fw_ref,
                xo_ref, h_ref, *, on_grid, final_norm):
    d = x_ref.shape[-1]
    tl = x_ref.shape[1]
    halo = xp_ref.shape[1]
    d_ff = wd_ref.shape[0]
    j = pl.program_id(1)
    m = m_ref[0]
    shift, scale, gate = m[:, 3 * d:4 * d], m[:, 4 * d:5 * d], m[:, 5 * d:6 * d]
    x = x_ref[0]
    h_ref[halo:halo + tl, :] = _rms_mod(x, nw_ref[...], shift, scale).astype(BF16)
    if on_grid:
        h_prev = _rms_mod(xp_ref[0], nw_ref[...], shift, scale)
        h_next = _rms_mod(xn_ref[0], nw_ref[...], shift, scale)
        h_ref[0:halo, :] = jnp.where(j > 0, h_prev, 0.0).astype(BF16)
        h_ref[halo + tl:, :] = jnp.where(j < pl.num_programs(1) - 1, h_next, 0.0).astype(BF16)
        period = GRID_W
    else:
        period = tl
    assert period & (period - 1) == 0
    col = jnp.bitwise_and(lax.broadcasted_iota(jnp.int32, (tl, 1), 0), period - 1)
    has_left = col != 0
    has_right = col != period - 1

    h_main = h_ref[halo:halo + tl, :]
    cw = cw_ref[...]
    if on_grid:
        a = _dot(h_ref[...], wu_ref[:, 0:d_ff].astype(BF16))
        mid = a[halo:halo + tl]
        up = a[halo - GRID_W:halo - GRID_W + tl]
        dn = a[halo + GRID_W:halo + GRID_W + tl]
    else:
        mid = _dot(h_main, wu_ref[:, 0:d_ff].astype(BF16))
    v = _dot(h_main, wu_ref[:, d_ff:2 * d_ff].astype(BF16))
    if on_grid:
        p = [cw[dx:dx + 1] * up + cw[3 + dx:4 + dx] * mid + cw[6 + dx:7 + dx] * dn for dx in range(CONV_W)]
    else:
        p = [cw[3 + dx:4 + dx] * mid for dx in range(CONV_W)]
    left = jnp.where(has_left, pltpu.roll(p[0], 1, axis=0), 0.0)
    right = jnp.where(has_right, pltpu.roll(p[2], tl - 1, axis=0), 0.0)
    conv = left + p[1] + right + cb_ref[...]
    act = _gelu_tanh_gate(conv, v).astype(BF16)
    out = x + gate * _dot(act, wd_ref[...].astype(BF16))
    if final_norm:
        out = out * lax.rsqrt(jnp.mean(out * out, axis=-1, keepdims=True) + RMS_EPS) * fw_ref[...]
    xo_ref[0] = out


def _conv_ffn(x, mod, norm_w, w_up, conv_w, conv_b, w_down, final_w, layer, on_grid, final_norm):
    bsz, length, d = x.shape
    tl = min(FFN_TILE, length)
    nt = length // tl
    assert on_grid or nt == 1, "the sequence form of the conv keeps the whole sequence in one tile"
    halo = GRID_W
    d_ff = w_down.shape[1]
    mod_rows = mod.shape[0]
    hb = tl // halo
    n_hb = length // halo
    const2 = lambda b, j: (0, 0)
    once = dict(pipeline_mode=pl.Buffered(1))
    return pl.pallas_call(
        functools.partial(_ffn_kernel, on_grid=on_grid, final_norm=final_norm),
        grid=(bsz, nt),
        in_specs=[pl.BlockSpec((1, halo, d), lambda b, j: (b, jnp.maximum(j * hb - 1, 0), 0)),
                  pl.BlockSpec((1, tl, d), lambda b, j: (b, j, 0)),
                  pl.BlockSpec((1, halo, d), lambda b, j: (b, jnp.minimum((j + 1) * hb, n_hb - 1), 0)),
                  pl.BlockSpec((1, 1, N_MOD * d), lambda b, j: (b % mod_rows, 0, 0)),
                  pl.BlockSpec((1, d), const2),
                  pl.BlockSpec((None, d, 2 * d_ff), lambda b, j: (layer, 0, 0), **once),
                  pl.BlockSpec((CONV_W * CONV_W, d_ff), const2, **once),
                  pl.BlockSpec((1, d_ff), const2, **once),
                  pl.BlockSpec((None, d_ff, d), lambda b, j: (layer, 0, 0), **once),
                  pl.BlockSpec((1, d), const2)],
        out_specs=pl.BlockSpec((1, tl, d), lambda b, j: (b, j, 0)),
        out_shape=jax.ShapeDtypeStruct((bsz, length, d), F32),
        scratch_shapes=[pltpu.VMEM((tl + 2 * halo, d), BF16)],
        compiler_params=pltpu.CompilerParams(dimension_semantics=("arbitrary", "arbitrary"),
                                             vmem_limit_bytes=VMEM_LIMIT_BYTES),
        name="conv_ffn",
    )(x, x, x, mod, norm_w, w_up, conv_w, conv_b, w_down, final_w)


def kernel(x, c, ctx, c_ctx, ada_w, ada_b, norm1_w, w_in, sgu_ln_w, sgu_ln_b, sgu_w, sgu_b,
           hgrn_lower_bounds, hgrn_norm_w, w_branch_a, w_branch_b, w_out, norm2_w,
           ffn_w_up, ffn_conv_w, ffn_conv_b, ffn_w_down, final_norm_w):
    depth = ada_w.shape[0]
    bsz, _, d = x.shape
    heads = d // HEAD_DIM
    d_ff = ffn_w_down.shape[1]
    assert (N_MOD * d) % MOD_COLS == 0

    pad = (-(bsz + 1)) % 8
    cvec = jnp.concatenate([c, c_ctx[None, :], jnp.zeros((pad, d), F32)], axis=0)
    mod = _modulation(cvec, ada_w, ada_b)
    mod_x = mod[:, :bsz, None, :]
    mod_c = mod[:, bsz:bsz + 1, None, :]

    hlb_f = hgrn_lower_bounds[:, :d]
    hlb_b = hgrn_lower_bounds[:, d:]
    zero_state = jnp.zeros((bsz, heads, HEAD_DIM, HEAD_DIM), F32)
    row = lambda t: t[None, :]

    for l in range(depth):
        last = l == depth - 1
        wl = w_in[l].astype(BF16)
        sw = sgu_w[l].astype(BF16)
        sgu_bias = jnp.repeat(sgu_b[l].T, d // SGU_GROUPS, axis=1)
        hnw = jnp.tile(hgrn_norm_w[l], heads)[None, :]
        w_a, w_b, w_o = (t[l].astype(BF16) for t in (w_branch_a, w_branch_b, w_out))
        conv_w = ffn_conv_w[l].reshape(CONV_W * CONV_W, d_ff)
        conv_b = ffn_conv_b[l][None, :]

        mixer = functools.partial(_forward_sweep, ln_w=row(sgu_ln_w[l]), ln_b=row(sgu_ln_b[l]), sgu_w=sw,
                                  sgu_bias=sgu_bias, hnorm_w=hnw, w_a=w_a, w_b=w_b, w_o=w_o, layer=l)
        ffn = functools.partial(_conv_ffn, norm_w=row(norm2_w[l]), w_up=ffn_w_up, conv_w=conv_w,
                                conv_b=conv_b, w_down=ffn_w_down, final_w=row(final_norm_w), layer=l)

        if last:
            s_b = _state_sweep(ctx, mod_c[l], row(norm1_w[l]), wl, hlb_b, zero_state, l, reverse=True)
            s_f = _state_sweep(ctx, mod_c[l], row(norm1_w[l]), wl, hlb_f, zero_state, l, reverse=False)
        else:
            ob_c, q_c, i_c, s_b = _reverse_sweep(ctx, mod_c[l], row(norm1_w[l]), wl, hlb_b, zero_state, l)
            ctx_mixed, s_f = mixer(ctx, mod_c[l], row(norm1_w[l]), wl, hlb_f, zero_state, q_c, i_c, ob_c)
        ob_x, q_x, i_x, _ = _reverse_sweep(x, mod_x[l], row(norm1_w[l]), wl, hlb_b, s_b, l)
        x, _ = mixer(x, mod_x[l], row(norm1_w[l]), wl, hlb_f, s_f, q_x, i_x, ob_x)
        x = ffn(x, mod_x[l], on_grid=True, final_norm=last)
        if not last:
            ctx = ffn(ctx_mixed, mod_c[l], on_grid=False, final_norm=False)
    return x
```

```python
import functools

import jax
import jax.numpy as jnp
from jax import lax
from jax.experimental import pallas as pl
from jax.experimental.pallas import tpu as pltpu

F32 = jnp.float32
BF16 = jnp.bfloat16

N_MOD = 6
RMS_EPS = 1e-6
LN_EPS = 1e-5
HEAD_DIM = 128
HGRN_CHUNK = 64
CUMSUM_ROWS = 256
SGU_CHUNK = 128
SGU_GROUPS = 8
GRID_W = 64
CONV_W = 3
IN_Q, IN_F_FWD, IN_F_BWD, IN_I, IN_U, IN_V, IN_OG, IN_GATE_A, IN_GATE_B = range(9)

TOKEN_TILE = 512
FFN_TILE = 512
MOD_COLS = 1536
VMEM_LIMIT_BYTES = 56 * 1024 * 1024


def _dot(a, b):
    return jnp.dot(a, b, preferred_element_type=F32)


def _sigmoid(x):
    return 1.0 / (1.0 + jnp.exp(-x))


def _silu(x):
    return x * _sigmoid(x)


GELU_C0 = 0.7978845608028654
GELU_C1 = GELU_C0 * 0.044715


def _gelu_tanh_gate(x, gate):
    hg = (0.5 * x) * gate
    return hg * jnp.tanh(x * ((x * x) * GELU_C1 + GELU_C0)) + hg


def _gelu_tanh(x):
    hx = 0.5 * x
    return hx * jnp.tanh(x * ((x * x) * GELU_C1 + GELU_C0)) + hx


def _rms_mod(x, norm_w, shift, scale):
    y = x * lax.rsqrt(jnp.mean(x * x, axis=-1, keepdims=True) + RMS_EPS) * norm_w
    return y * (1.0 + scale) + shift


def _lower_bound_row(hlb, layer):
    depth = hlb.shape[0]
    rows = [hlb[j:j + 1] for j in range(depth)]
    mx = functools.reduce(jnp.maximum, rows)
    es = [jnp.exp(r - mx) for r in rows]
    tot = functools.reduce(lambda a, b: a + b, es)
    acc = jnp.zeros_like(rows[0])
    for j in range(1, layer + 1):
        acc = acc + es[j] / tot
    return acc


def _hgrn_tile(q, z, v, lb, st_ref, reverse, need_out=True):
    tl, d = z.shape
    n_chunks = tl // HGRN_CHUNK
    assert n_chunks % 2 == 0
    shift = HGRN_CHUNK.bit_length() - 1
    cs = min(tl, CUMSUM_ROWS)
    r = lax.broadcasted_iota(jnp.int32, (cs, cs), 0)
    c = lax.broadcasted_iota(jnp.int32, (cs, cs), 1)
    same_chunk = jnp.right_shift(r, shift) == jnp.right_shift(c, shift)
    ordered = (c >= r) if reverse else (c <= r)
    tri = jnp.where(jnp.logical_and(same_chunk, ordered), 1.0, 0.0).astype(BF16)
    pair = 2 * HGRN_CHUNK
    pair_mask = jnp.logical_and(same_chunk, ordered)[:pair, :pair]

    e = jnp.exp(-z)
    sig = 1.0 / (1.0 + e)
    f = lb + (1.0 - lb) * sig
    k = 1.0 - f
    g = jnp.log(f)
    g_hi = g.astype(BF16)
    g_lo = (g - g_hi.astype(F32)).astype(BF16)
    b = jnp.concatenate([_dot(tri, g_hi[i:i + cs]) + _dot(tri, g_lo[i:i + cs]) for i in range(0, tl, cs)], axis=0)
    half = HGRN_CHUNK // 2
    ref_i, last_i = (half, 0) if reverse else (half - 1, HGRN_CHUNK - 1)
    refs = [b[n * HGRN_CHUNK + ref_i:n * HGRN_CHUNK + ref_i + 1] for n in range(n_chunks)]
    lasts = [b[n * HGRN_CHUNK + last_i:n * HGRN_CHUNK + last_i + 1] for n in range(n_chunks)]
    ref_rows = jnp.concatenate([jnp.broadcast_to(t, (HGRN_CHUNK, d)) for t in refs], axis=0)
    qs = (q * jnp.exp(b - ref_rows)).astype(BF16) if need_out else None
    ksf = k * jnp.exp(ref_rows - b)
    ks = ksf.astype(BF16)
    vb = v.astype(BF16)

    heads = d // HEAD_DIM
    intra = [None] * n_chunks
    if need_out:
        for n in range(0, n_chunks, 2):
            rs2 = slice(n * HGRN_CHUNK, (n + 2) * HGRN_CHUNK)
            cols2 = []
            for h in range(heads):
                sl = slice(h * HEAD_DIM, (h + 1) * HEAD_DIM)
                sc = _dot(qs[rs2, sl], ksf[rs2, sl].T.astype(BF16))
                cols2.append(_dot(jnp.where(pair_mask, sc, 0.0).astype(BF16), vb[rs2, sl]))
            both = jnp.concatenate(cols2, axis=-1)
            intra[n], intra[n + 1] = both[:HGRN_CHUNK], both[HGRN_CHUNK:]
    states = [st_ref[h] for h in range(heads)]
    outs = [None] * n_chunks
    for n in (reversed(range(n_chunks)) if reverse else range(n_chunks)):
        rs = slice(n * HGRN_CHUNK, (n + 1) * HGRN_CHUNK)
        e_ref = jnp.exp(refs[n]) if need_out else None
        e_last_ref = jnp.exp(lasts[n] - refs[n])
        decay = jnp.exp(lasts[n])
        cols = []
        lanes = [slice(h * HEAD_DIM, (h + 1) * HEAD_DIM) for h in range(heads)]
        kv_t = [_dot(v[rs, sl].T.astype(BF16), ks[rs, sl]) for sl in lanes]
        for h in range(heads):
            sl = lanes[h]
            if need_out:
                cols.append(_dot(qs[rs, sl], (states[h] * e_ref[:, sl]).T.astype(BF16)))
            states[h] = states[h] * decay[:, sl] + kv_t[h] * e_last_ref[:, sl]
        if need_out:
            outs[n] = jnp.concatenate(cols, axis=-1) + intra[n]
    for h in range(heads):
        st_ref[h] = states[h]
    return jnp.concatenate(outs, axis=0) if need_out else None


def _mod_kernel(c_ref, w_ref, b_ref, o_ref):
    s = _silu(c_ref[...])
    w = w_ref[0]
    s_hi = s.astype(BF16)
    s_lo = (s - s_hi.astype(F32)).astype(BF16)
    w_hi = w.astype(BF16)
    w_lo = (w - w_hi.astype(F32)).astype(BF16)
    o_ref[0] = _dot(s_hi, w_hi) + _dot(s_lo, w_hi) + _dot(s_hi, w_lo) + b_ref[0]


def _modulation(cvec, ada_w, ada_b):
    depth, d, n = ada_w.shape
    rows = cvec.shape[0]
    return pl.pallas_call(
        _mod_kernel,
        grid=(depth, n // MOD_COLS),
        in_specs=[pl.BlockSpec((rows, d), lambda l, j: (0, 0)),
                  pl.BlockSpec((1, d, MOD_COLS), lambda l, j: (l, 0, j)),
                  pl.BlockSpec((1, 1, MOD_COLS), lambda l, j: (l, 0, j))],
        out_specs=pl.BlockSpec((1, rows, MOD_COLS), lambda l, j: (l, 0, j)),
        out_shape=jax.ShapeDtypeStruct((depth, rows, n), F32),
        compiler_params=pltpu.CompilerParams(dimension_semantics=("arbitrary", "arbitrary"),
                                             vmem_limit_bytes=VMEM_LIMIT_BYTES),
        name="adaln_modulation",
    )(cvec, ada_w, ada_b.reshape(depth, 1, n))


def _rev_kernel(x_ref, m_ref, nw_ref, wq_ref, wfb_ref, wi_ref, hlb_ref, s0_ref,
                ob_ref, q_ref, i_ref, st_out_ref,
                st_ref, *, layer):
    d = x_ref.shape[-1]
    j = pl.program_id(1)

    @pl.when(j == 0)
    def _():
        st_ref[...] = s0_ref[0]

    m = m_ref[0]
    h = _rms_mod(x_ref[0], nw_ref[...], m[:, 0:d], m[:, d:2 * d]).astype(BF16)
    q = _silu(_dot(h, wq_ref[...]))
    q_ref[0] = q.astype(BF16)
    zb = _dot(h, wfb_ref[...])
    iv = _dot(h, wi_ref[...])
    i_ref[0] = iv.astype(BF16)
    lb = _lower_bound_row(hlb_ref[...], layer)
    ob_ref[0] = _hgrn_tile(q, zb, iv, lb, st_ref, reverse=True)

    @pl.when(j == pl.num_programs(1) - 1)
    def _():
        st_out_ref[0] = st_ref[...]


def _reverse_sweep(x, mod, norm_w, w_in, hlb, s0, layer):
    bsz, length, d = x.shape
    tl = min(TOKEN_TILE, length)
    nt = length // tl
    heads = d // HEAD_DIM
    mod_rows = mod.shape[0]
    rev = lambda b, j: (b, nt - 1 - j, 0)
    const2 = lambda b, j: (0, 0)
    w_block = lambda n: pl.BlockSpec((None, d, d), lambda b, j: (layer, 0, n))
    state_spec = pl.BlockSpec((1, heads, HEAD_DIM, HEAD_DIM), lambda b, j: (b, 0, 0, 0))
    return pl.pallas_call(
        functools.partial(_rev_kernel, layer=layer),
        grid=(bsz, nt),
        in_specs=[pl.BlockSpec((1, tl, d), rev),
                  pl.BlockSpec((1, 1, N_MOD * d), lambda b, j: (b % mod_rows, 0, 0)),
                  pl.BlockSpec((1, d), const2),
                  w_block(IN_Q), w_block(IN_F_BWD), w_block(IN_I),
                  pl.BlockSpec(hlb.shape, const2),
                  state_spec],
        out_specs=[pl.BlockSpec((1, tl, d), rev),
                   pl.BlockSpec((1, tl, d), rev),
                   pl.BlockSpec((1, tl, d), rev),
                   state_spec],
        out_shape=[jax.ShapeDtypeStruct((bsz, length, d), F32),
                   jax.ShapeDtypeStruct((bsz, length, d), BF16),
                   jax.ShapeDtypeStruct((bsz, length, d), BF16),
                   jax.ShapeDtypeStruct((bsz, heads, HEAD_DIM, HEAD_DIM), F32)],
        scratch_shapes=[pltpu.VMEM((heads, HEAD_DIM, HEAD_DIM), F32)],
        compiler_params=pltpu.CompilerParams(dimension_semantics=("arbitrary", "arbitrary"),
                                             vmem_limit_bytes=VMEM_LIMIT_BYTES),
        name="hgrn_reverse_sweep",
    )(x, mod, norm_w, w_in, w_in, w_in, hlb, s0)


def _state_kernel(xf_ref, xb_ref, m_ref, nw_ref, wff_ref, wfb_ref, wi_ref, hlbf_ref, hlbb_ref, s0f_ref, s0b_ref,
                  stf_out_ref, stb_out_ref, stf_ref, stb_ref, *, layer, same_tile):
    d = xf_ref.shape[-1]
    j = pl.program_id(1)

    @pl.when(j == 0)
    def _():
        stf_ref[...] = s0f_ref[0]
        stb_ref[...] = s0b_ref[0]

    m = m_ref[0]
    hf = _rms_mod(xf_ref[0], nw_ref[...], m[:, 0:d], m[:, d:2 * d]).astype(BF16)
    i_f = _dot(hf, wi_ref[...])
    if same_tile:
        hb, i_b = hf, i_f
    else:
        hb = _rms_mod(xb_ref[0], nw_ref[...], m[:, 0:d], m[:, d:2 * d]).astype(BF16)
        i_b = _dot(hb, wi_ref[...])
    _hgrn_tile(None, _dot(hf, wff_ref[...]), i_f, _lower_bound_row(hlbf_ref[...], layer), stf_ref, False, need_out=False)
    _hgrn_tile(None, _dot(hb, wfb_ref[...]), i_b, _lower_bound_row(hlbb_ref[...], layer), stb_ref, True, need_out=False)

    @pl.when(j == pl.num_programs(1) - 1)
    def _():
        stf_out_ref[0] = stf_ref[...]
        stb_out_ref[0] = stb_ref[...]


def _state_sweep(x, mod, norm_w, w_in, hlb_f, hlb_b, s0_f, s0_b, layer):
    bsz, length, d = x.shape
    tl = min(TOKEN_TILE, length)
    nt = length // tl
    heads = d // HEAD_DIM
    mod_rows = mod.shape[0]
    const2 = lambda b, j: (0, 0)
    w_block = lambda n: pl.BlockSpec((None, d, d), lambda b, j: (layer, 0, n))
    state_spec = pl.BlockSpec((1, heads, HEAD_DIM, HEAD_DIM), lambda b, j: (b, 0, 0, 0))
    state_shape = jax.ShapeDtypeStruct((bsz, heads, HEAD_DIM, HEAD_DIM), F32)
    return pl.pallas_call(
        functools.partial(_state_kernel, layer=layer, same_tile=nt == 1),
        grid=(bsz, nt),
        in_specs=[pl.BlockSpec((1, tl, d), lambda b, j: (b, j, 0)),
                  pl.BlockSpec((1, tl, d), lambda b, j: (b, nt - 1 - j, 0)),
                  pl.BlockSpec((1, 1, N_MOD * d), lambda b, j: (b % mod_rows, 0, 0)),
                  pl.BlockSpec((1, d), const2),
                  w_block(IN_F_FWD), w_block(IN_F_BWD), w_block(IN_I),
                  pl.BlockSpec(hlb_f.shape, const2),
                  pl.BlockSpec(hlb_b.shape, const2),
                  state_spec, state_spec],
        out_specs=[state_spec, state_spec],
        out_shape=[state_shape, state_shape],
        scratch_shapes=[pltpu.VMEM((heads, HEAD_DIM, HEAD_DIM), F32),
                        pltpu.VMEM((heads, HEAD_DIM, HEAD_DIM), F32)],
        compiler_params=pltpu.CompilerParams(dimension_semantics=("arbitrary", "arbitrary"),
                                             vmem_limit_bytes=VMEM_LIMIT_BYTES),
        name="hgrn_state_sweep",
    )(x, x, mod, norm_w, w_in, w_in, w_in, hlb_f, hlb_b, s0_f, s0_b)


def _fwd_kernel(x_ref, m_ref, nw_ref, wff_ref, wu_ref, wv_ref, wog_ref, wga_ref, wgb_ref,
                hlb_ref, s0_ref, q_ref, i_ref, ob_ref,
                lnw_ref, lnb_ref, sw_ref, sb_ref, hnw_ref, wa_ref, wb_ref, wo_ref,
                xo_ref, st_out_ref,
                st_ref, *, layer):
    d = x_ref.shape[-1]
    tl = x_ref.shape[1]
    j = pl.program_id(1)

    @pl.when(j == 0)
    def _():
        st_ref[...] = s0_ref[0]

    x = x_ref[0]
    m = m_ref[0]
    h = _rms_mod(x, nw_ref[...], m[:, 0:d], m[:, d:2 * d]).astype(BF16)

    lb = _lower_bound_row(hlb_ref[...], layer)
    z_f = _dot(h, wff_ref[...])
    v_proj = _dot(h, wv_ref[...])
    o_f = _hgrn_tile(q_ref[0].astype(F32), z_f, i_ref[0].astype(F32), lb, st_ref, reverse=False)

    og = _dot(h, wog_ref[...])
    u = _dot(h, wu_ref[...])
    gate_a = _dot(h, wga_ref[...])
    gate_b = _dot(h, wgb_ref[...])

    o = o_f + ob_ref[0]
    parts = []
    for hh in range(d // HEAD_DIM):
        oh = o[:, hh * HEAD_DIM:(hh + 1) * HEAD_DIM]
        parts.append(oh * lax.rsqrt(jnp.mean(oh * oh, axis=-1, keepdims=True) + RMS_EPS))
    y_b = (jnp.concatenate(parts, axis=-1) * hnw_ref[...]) * _silu(og)
    t_b = _dot(y_b.astype(BF16), wb_ref[...])

    gv = _gelu_tanh(v_proj)
    mu = jnp.mean(gv, axis=-1, keepdims=True)
    cen = gv - mu
    var = jnp.mean(cen * cen, axis=-1, keepdims=True)
    vn = (cen * lax.rsqrt(var + LN_EPS) * lnw_ref[...] + lnb_ref[...]).astype(BF16)
    gd = d // SGU_GROUPS
    row_blocks = []
    for n in range(0, tl // SGU_CHUNK, 2):
        r0 = slice(n * SGU_CHUNK, (n + 1) * SGU_CHUNK)
        r1 = slice((n + 1) * SGU_CHUNK, (n + 2) * SGU_CHUNK)
        two = [_dot(sw_ref[g], jnp.concatenate([vn[r0, g * gd:(g + 1) * gd], vn[r1, g * gd:(g + 1) * gd]], axis=-1))
               for g in range(SGU_GROUPS)]
        row_blocks.append(jnp.concatenate([t[:, :gd] for t in two], axis=-1) + sb_ref[...])
        row_blocks.append(jnp.concatenate([t[:, gd:] for t in two], axis=-1) + sb_ref[...])
    mixed = jnp.concatenate(row_blocks, axis=0)
    y_a = _gelu_tanh_gate(u, mixed)
    t_a = _dot(y_a.astype(BF16), wa_ref[...])

    merged = _sigmoid(gate_a) * t_a + _sigmoid(gate_b) * t_b
    y = _dot(merged.astype(BF16), wo_ref[...])
    xo_ref[0] = x + m[:, 2 * d:3 * d] * y

    @pl.when(j == pl.num_programs(1) - 1)
    def _():
        st_out_ref[0] = st_ref[...]


def _forward_sweep(x, mod, norm_w, w_in, hlb, s0, q, iv, ob, ln_w, ln_b, sgu_w, sgu_bias,
                   hnorm_w, w_a, w_b, w_o, layer):
    bsz, length, d = x.shape
    tl = min(TOKEN_TILE, length)
    nt = length // tl
    heads = d // HEAD_DIM
    mod_rows = mod.shape[0]
    tok = lambda b, j: (b, j, 0)
    const2 = lambda b, j: (0, 0)
    const3 = lambda b, j: (0, 0, 0)
    once = dict(pipeline_mode=pl.Buffered(1))
    w_block = lambda n: pl.BlockSpec((None, d, d), lambda b, j: (layer, 0, n), **once)
    w_square = pl.BlockSpec((None, d, d), lambda b, j: (layer, 0, 0), **once)
    state_spec = pl.BlockSpec((1, heads, HEAD_DIM, HEAD_DIM), lambda b, j: (b, 0, 0, 0))
    return pl.pallas_call(
        functools.partial(_fwd_kernel, layer=layer),
        grid=(bsz, nt),
        in_specs=[pl.BlockSpec((1, tl, d), tok),
                  pl.BlockSpec((1, 1, N_MOD * d), lambda b, j: (b % mod_rows, 0, 0)),
                  pl.BlockSpec((1, d), const2),
                  w_block(IN_F_FWD), w_block(IN_U), w_block(IN_V), w_block(IN_OG), w_block(IN_GATE_A),
                  w_block(IN_GATE_B),
                  pl.BlockSpec(hlb.shape, const2),
                  state_spec,
                  pl.BlockSpec((1, tl, d), tok),
                  pl.BlockSpec((1, tl, d), tok),
                  pl.BlockSpec((1, tl, d), tok),
                  pl.BlockSpec((1, d), const2),
                  pl.BlockSpec((1, d), const2),
                  pl.BlockSpec(sgu_w.shape, const3, **once),
                  pl.BlockSpec(sgu_bias.shape, const2, **once),
                  pl.BlockSpec((1, d), const2),
                  w_square, w_square, w_square],
        out_specs=[pl.BlockSpec((1, tl, d), tok), state_spec],
        out_shape=[jax.ShapeDtypeStruct((bsz, length, d), F32),
                   jax.ShapeDtypeStruct((bsz, heads, HEAD_DIM, HEAD_DIM), F32)],
        scratch_shapes=[pltpu.VMEM((heads, HEAD_DIM, HEAD_DIM), F32)],
        compiler_params=pltpu.CompilerParams(dimension_semantics=("arbitrary", "arbitrary"),
                                             vmem_limit_bytes=VMEM_LIMIT_BYTES),
        name="token_mixer_forward_sweep",
    )(x, mod, norm_w, w_in, w_in, w_in, w_in, w_in, w_in, hlb, s0, q, iv, ob, ln_w, ln_b, sgu_w, sgu_bias,
      hnorm_w, w_a, w_b, w_o)


def _ffn_kernel(xp_ref, x_ref, xn_ref, m_ref, nw_ref, wu_ref, cw_ref, cb_ref, wd_ref, fw_ref,
                xo_ref, h_ref, *, on_grid, final_norm):
    d = x_ref.shape[-1]
    tl = x_ref.shape[1]
    halo = xp_ref.shape[1]
    d_ff = wd_ref.shape[0]
    j = pl.program_id(1)
    m = m_ref[0]
    shift, scale, gate = m[:, 3 * d:4 * d], m[:, 4 * d:5 * d], m[:, 5 * d:6 * d]
    x = x_ref[0]
    h_ref[halo:halo + tl, :] = _rms_mod(x, nw_ref[...], shift, scale).astype(BF16)
    if on_grid:
        h_prev = _rms_mod(xp_ref[0], nw_ref[...], shift, scale)
        h_next = _rms_mod(xn_ref[0], nw_ref[...], shift, scale)
        h_ref[0:halo, :] = jnp.where(j > 0, h_prev, 0.0).astype(BF16)
        h_ref[halo + tl:, :] = jnp.where(j < pl.num_programs(1) - 1, h_next, 0.0).astype(BF16)
        period = GRID_W
    else:
        period = tl
    assert period & (period - 1) == 0
    col = jnp.bitwise_and(lax.broadcasted_iota(jnp.int32, (tl, 1), 0), period - 1)
    has_left = col != 0
    has_right = col != period - 1

    h_main = h_ref[halo:halo + tl, :]
    cw = cw_ref[...]
    if on_grid:
        a = _dot(h_ref[...], wu_ref[:, 0:d_ff])
        mid = a[halo:halo + tl]
        up = a[halo - GRID_W:halo - GRID_W + tl]
        dn = a[halo + GRID_W:halo + GRID_W + tl]
    else:
        mid = _dot(h_main, wu_ref[:, 0:d_ff])
    v = _dot(h_main, wu_ref[:, d_ff:2 * d_ff])
    if on_grid:
        p = [cw[dx:dx + 1] * up + cw[3 + dx:4 + dx] * mid + cw[6 + dx:7 + dx] * dn for dx in range(CONV_W)]
    else:
        p = [cw[3 + dx:4 + dx] * mid for dx in range(CONV_W)]
    left = jnp.where(has_left, pltpu.roll(p[0], 1, axis=0), 0.0)
    right = jnp.where(has_right, pltpu.roll(p[2], tl - 1, axis=0), 0.0)
    conv = left + p[1] + right + cb_ref[...]
    act = _gelu_tanh_gate(conv, v).astype(BF16)
    out = x + gate * _dot(act, wd_ref[...])
    if final_norm:
        out = out * lax.rsqrt(jnp.mean(out * out, axis=-1, keepdims=True) + RMS_EPS) * fw_ref[...]
    xo_ref[0] = out


def _conv_ffn(x, mod, norm_w, w_up, conv_w, conv_b, w_down, final_w, layer, on_grid, final_norm):
    bsz, length, d = x.shape
    tl = min(FFN_TILE, length)
    nt = length // tl
    assert on_grid or nt == 1, "the sequence form of the conv keeps the whole sequence in one tile"
    halo = GRID_W
    d_ff = w_down.shape[1]
    mod_rows = mod.shape[0]
    hb = tl // halo
    n_hb = length // halo
    const2 = lambda b, j: (0, 0)
    once = dict(pipeline_mode=pl.Buffered(1))
    return pl.pallas_call(
        functools.partial(_ffn_kernel, on_grid=on_grid, final_norm=final_norm),
        grid=(bsz, nt),
        in_specs=[pl.BlockSpec((1, halo, d), lambda b, j: (b, jnp.maximum(j * hb - 1, 0), 0)),
                  pl.BlockSpec((1, tl, d), lambda b, j: (b, j, 0)),
                  pl.BlockSpec((1, halo, d), lambda b, j: (b, jnp.minimum((j + 1) * hb, n_hb - 1), 0)),
                  pl.BlockSpec((1, 1, N_MOD * d), lambda b, j: (b % mod_rows, 0, 0)),
                  pl.BlockSpec((1, d), const2),
                  pl.BlockSpec((None, d, 2 * d_ff), lambda b, j: (layer, 0, 0), **once),
                  pl.BlockSpec((CONV_W * CONV_W, d_ff), const2, **once),
                  pl.BlockSpec((1, d_ff), const2, **once),
                  pl.BlockSpec((None, d_ff, d), lambda b, j: (layer, 0, 0), **once),
                  pl.BlockSpec((1, d), const2)],
        out_specs=pl.BlockSpec((1, tl, d), lambda b, j: (b, j, 0)),
        out_shape=jax.ShapeDtypeStruct((bsz, length, d), F32),
        scratch_shapes=[pltpu.VMEM((tl + 2 * halo, d), BF16)],
        compiler_params=pltpu.CompilerParams(dimension_semantics=("arbitrary", "arbitrary"),
                                             vmem_limit_bytes=VMEM_LIMIT_BYTES),
        name="conv_ffn",
    )(x, x, x, mod, norm_w, w_up, conv_w, conv_b, w_down, final_w)


def kernel(x, c, ctx, c_ctx, ada_w, ada_b, norm1_w, w_in, sgu_ln_w, sgu_ln_b, sgu_w, sgu_b,
           hgrn_lower_bounds, hgrn_norm_w, w_branch_a, w_branch_b, w_out, norm2_w,
           ffn_w_up, ffn_conv_w, ffn_conv_b, ffn_w_down, final_norm_w):
    depth = ada_w.shape[0]
    bsz, _, d = x.shape
    heads = d // HEAD_DIM
    d_ff = ffn_w_down.shape[1]
    assert (N_MOD * d) % MOD_COLS == 0

    pad = (-(bsz + 1)) % 8
    cvec = jnp.concatenate([c, c_ctx[None, :], jnp.zeros((pad, d), F32)], axis=0)
    mod = _modulation(cvec, ada_w, ada_b)
    mod_x = mod[:, :bsz, None, :]
    mod_c = mod[:, bsz:bsz + 1, None, :]

    hlb_f = hgrn_lower_bounds[:, :d]
    hlb_b = hgrn_lower_bounds[:, d:]
    zero_state = jnp.zeros((bsz, heads, HEAD_DIM, HEAD_DIM), F32)
    row = lambda t: t[None, :]

    wl = w_in.astype(BF16)
    w_a, w_b, w_o, w_up, w_down = (t.astype(BF16) for t in (w_branch_a, w_branch_b, w_out, ffn_w_up, ffn_w_down))

    for l in range(depth):
        last = l == depth - 1
        sw = sgu_w[l].astype(BF16)
        sgu_bias = jnp.repeat(sgu_b[l].T, d // SGU_GROUPS, axis=1)
        hnw = jnp.tile(hgrn_norm_w[l], heads)[None, :]
        conv_w = ffn_conv_w[l].reshape(CONV_W * CONV_W, d_ff)
        conv_b = ffn_conv_b[l][None, :]

        mixer = functools.partial(_forward_sweep, ln_w=row(sgu_ln_w[l]), ln_b=row(sgu_ln_b[l]), sgu_w=sw,
                                  sgu_bias=sgu_bias, hnorm_w=hnw, w_a=w_a, w_b=w_b, w_o=w_o, layer=l)
        ffn = functools.partial(_conv_ffn, norm_w=row(norm2_w[l]), w_up=w_up, conv_w=conv_w,
                                conv_b=conv_b, w_down=w_down, final_w=row(final_norm_w), layer=l)

        if last:
            s_f, s_b = _state_sweep(ctx, mod_c[l], row(norm1_w[l]), wl, hlb_f, hlb_b, zero_state, zero_state, l)
        else:
            ob_c, q_c, i_c, s_b = _reverse_sweep(ctx, mod_c[l], row(norm1_w[l]), wl, hlb_b, zero_state, l)
            ctx_mixed, s_f = mixer(ctx, mod_c[l], row(norm1_w[l]), wl, hlb_f, zero_state, q_c, i_c, ob_c)
        ob_x, q_x, i_x, _ = _reverse_sweep(x, mod_x[l], row(norm1_w[l]), wl, hlb_b, s_b, l)
        x, _ = mixer(x, mod_x[l], row(norm1_w[l]), wl, hlb_f, s_f, q_x, i_x, ob_x)
        x = ffn(x, mod_x[l], on_grid=True, final_norm=last)
        if not last:
            ctx = ffn(ctx_mixed, mod_c[l], on_grid=False, final_norm=False)
    return x
```
